```python
import math
import jax, jax.numpy as jnp
from jax import lax
import numpy as np

D_MODEL = 2048
BATCH = 4
SEQ = 4096
DEPTH = 1

SSM_HEAD_DIM = 64
SSM_HEADS = D_MODEL // SSM_HEAD_DIM
D_SSM = SSM_HEADS * SSM_HEAD_DIM
SSM_GROUPS = 8
HEADS_PER_GROUP = SSM_HEADS // SSM_GROUPS
D_STATE = 128
CONV_WIDTH = 4
SSD_CHUNK = 128
D_XBC = D_SSM + 2 * SSM_GROUPS * D_STATE

ATT_HEAD_DIM = 128
ATT_HEADS = D_MODEL // ATT_HEAD_DIM
D_ATT = ATT_HEADS * ATT_HEAD_DIM
DILATION_PAIRS = ((128, 1), (512, 4), (2048, 16))
ATT_BLOCK = 128

D_MIX = D_SSM + D_ATT
IN_SPLITS = (D_SSM,
             D_SSM + D_XBC,
             D_SSM + D_XBC + SSM_HEADS,
             D_SSM + D_XBC + SSM_HEADS + D_ATT,
             D_SSM + D_XBC + SSM_HEADS + 2 * D_ATT)
D_IN_PROJ = D_SSM + D_XBC + SSM_HEADS + 3 * D_ATT
D_FF = 4 * D_MODEL
EPS = 1e-6

kernel_name = "hymba_ssd_dilated_swa_sqrelu"


def rmsnorm(x, w):
    xf = x.astype(jnp.float32)
    xf = xf * lax.rsqrt(jnp.mean(xf * xf, axis=-1, keepdims=True) + EPS)
    return (xf * w.astype(jnp.float32)).astype(x.dtype)


def causal_depthwise_conv(u, w, b):
    out = lax.conv_general_dilated(
        u, w[:, None, :].astype(u.dtype), window_strides=(1,),
        padding=[(CONV_WIDTH - 1, 0)],
        dimension_numbers=('NWC', 'WIO', 'NWC'),
        feature_group_count=u.shape[-1])
    return out + b.astype(u.dtype)


def ssd_chunked(xh, dt, a, bm, cm):
    b_, s_ = xh.shape[:2]
    nc = s_ // SSD_CHUNK

    def chunk(t):
        return t.reshape((b_, nc, SSD_CHUNK) + t.shape[2:])

    xc, dtc, bc, cc = chunk(xh), chunk(dt), chunk(bm), chunk(cm)
    a_cs = jnp.cumsum(dtc * a, axis=2).transpose(0, 1, 3, 4, 2)
    causal = jnp.tril(jnp.ones((SSD_CHUNK, SSD_CHUNK), dtype=bool))
    decay_in = jnp.exp(jnp.where(causal, a_cs[..., :, None] - a_cs[..., None, :], -jnp.inf))
    cb = jnp.einsum('bcign,bcjgn->bcgij', cc, bc)
    xdt = xc * dtc[..., None]
    y_diag = jnp.einsum('bcgrij,bcjgrp->bcigrp', cb[:, :, :, None] * decay_in, xdt)

    decay_to_end = jnp.exp(a_cs[..., -1:] - a_cs)
    states = jnp.einsum('bcjgn,bcgrj,bcjgrp->bcgrpn', bc, decay_to_end, xdt)
    chunk_decay = jnp.exp(a_cs[..., -1])

    def step(h, inp):
        st, dec = inp
        return h * dec[..., None, None] + st, h

    h0 = jnp.zeros(states.shape[:1] + states.shape[2:], jnp.float32)
    _, prev = lax.scan(step, h0, (jnp.swapaxes(states, 0, 1), jnp.swapaxes(chunk_decay, 0, 1)))
    prev = jnp.swapaxes(prev, 0, 1)
    y_off = jnp.einsum('bcign,bcgrpn,bcgri->bcigrp', cc, prev, jnp.exp(a_cs))
    return (y_diag + y_off).reshape(xh.shape)


def dilated_window_attention(q, k, v, window, dilation):
    b_, s_, h_, d_ = q.shape
    sd = s_ // dilation
    reach = window // dilation
    nb = -(-sd // ATT_BLOCK)
    lp = nb * ATT_BLOCK
    bd = b_ * dilation

    def decimate(t):
        t = t.reshape(b_, sd, dilation, h_, d_).transpose(0, 2, 1, 3, 4)
        t = t.reshape(bd, sd, h_, d_)
        return jnp.pad(t, ((0, 0), (0, lp - sd), (0, 0), (0, 0)))

    def with_prev(t):
        t = jnp.pad(t, ((0, 0), (ATT_BLOCK, 0), (0, 0), (0, 0)))
        t = t.reshape(bd, nb + 1, ATT_BLOCK, h_, d_)
        return jnp.concatenate([t[:, :-1], t[:, 1:]], axis=2)

    qb = decimate(q).reshape(bd, nb, ATT_BLOCK, h_, d_)
    kb = with_prev(decimate(k))
    vb = with_prev(decimate(v))
    s = jnp.einsum('bnqhd,bnkhd->bnhqk', qb, kb)
    qi = jnp.arange(ATT_BLOCK)[:, None]
    kj = jnp.arange(2 * ATT_BLOCK)[None, :]
    dist = ATT_BLOCK + qi - kj
    key_pos = (jnp.arange(nb)[:, None, None] - 1) * ATT_BLOCK + kj[None]
    mask = (dist >= 0) & (dist <= reach) & (key_pos >= 0)
    s = jnp.where(mask[None, :, None], s, -jnp.inf)
    m = jnp.max(s, axis=-1, keepdims=True)
    p = jnp.exp(s - m)
    den = jnp.sum(p, axis=-1)
    o = jnp.einsum('bnhqk,bnkhd->bnqhd', p, vb) / jnp.swapaxes(den, 2, 3)[..., None]
    lse = jnp.swapaxes(m[..., 0] + jnp.log(den), 2, 3)

    def undecimate(t):
        t = t.reshape((bd, lp) + t.shape[3:])[:, :sd]
        t = t.reshape((b_, dilation, sd) + t.shape[2:])
        return jnp.moveaxis(t, 1, 2).reshape((b_, s_) + t.shape[3:])

    return undecimate(o), undecimate(lse)


def hybrid_mixer(u, w_in, conv_w, conv_b, dt_bias, a_log, d_skip, ssm_norm_w, w_out):
    b_, s_, _ = u.shape
    f32 = jnp.float32
    proj = jnp.einsum('bsd,de->bse', u, w_in)
    z, xbc, dt_raw, q, k, v = jnp.split(proj, IN_SPLITS, axis=-1)

    xbc = jax.nn.silu(causal_depthwise_conv(xbc, conv_w, conv_b))
    xs, bm, cm = jnp.split(xbc, (D_SSM, D_SSM + SSM_GROUPS * D_STATE), axis=-1)
    xh = xs.astype(f32).reshape(b_, s_, SSM_GROUPS, HEADS_PER_GROUP, SSM_HEAD_DIM)
    dt = jax.nn.softplus(dt_raw.astype(f32) + dt_bias.astype(f32))
    dt = dt.reshape(b_, s_, SSM_GROUPS, HEADS_PER_GROUP)
    a = -jnp.exp(a_log.astype(f32)).reshape(SSM_GROUPS, HEADS_PER_GROUP)
    bm = bm.astype(f32).reshape(b_, s_, SSM_GROUPS, D_STATE)
    cm = cm.astype(f32).reshape(b_, s_, SSM_GROUPS, D_STATE)
    y = ssd_chunked(xh, dt, a, bm, cm)
    y = y + d_skip.astype(f32).reshape(SSM_GROUPS, HEADS_PER_GROUP)[:, :, None] * xh
    yg = y.reshape(b_, s_, SSM_GROUPS, -1) * jax.nn.silu(z.astype(f32)).reshape(b_, s_, SSM_GROUPS, -1)
    yg = yg * lax.rsqrt(jnp.mean(yg * yg, axis=-1, keepdims=True) + EPS)
    y_ssm = (yg.reshape(b_, s_, D_SSM) * ssm_norm_w.astype(f32)).astype(u.dtype)

    qh = q.astype(f32).reshape(b_, s_, ATT_HEADS, ATT_HEAD_DIM) * (ATT_HEAD_DIM ** -0.5)
    kh = k.astype(f32).reshape(b_, s_, ATT_HEADS, ATT_HEAD_DIM)
    vh = v.astype(f32).reshape(b_, s_, ATT_HEADS, ATT_HEAD_DIM)
    outs, lses = [], []
    for window, dilation in DILATION_PAIRS:
        o, l = dilated_window_attention(qh, kh, vh, window, dilation)
        outs.append(o)
        lses.append(l)
    wts = jax.nn.softmax(jnp.stack(lses, axis=0), axis=0)
    y_att = jnp.einsum('ibsh,ibshd->bshd', wts, jnp.stack(outs, axis=0))
    y_att = y_att.reshape(b_, s_, D_ATT).astype(u.dtype)

    y_mix = jnp.concatenate([y_ssm, y_att], axis=-1)
    return jnp.einsum('bse,ed->bsd', y_mix, w_out)


def squared_relu_mlp(u, w_up, w_down):
    hdn = jax.nn.relu(jnp.einsum('bsd,df->bsf', u, w_up))
    return jnp.einsum('bsf,fd->bsd', hdn * hdn, w_down)


def setup_inputs(seed: int = 0) -> dict:
    key = jax.random.key(seed)
    ks = jax.random.split(key, 16)
    L = DEPTH

    def gain(k, n):
        return 1.0 + 0.1 * jax.random.normal(k, (L, n), jnp.float32)

    dt0 = jnp.exp(jax.random.uniform(ks[5], (L, SSM_HEADS), jnp.float32,
                                     math.log(1e-3), math.log(1e-1)))
    dt_bias = dt0 + jnp.log(-jnp.expm1(-dt0))
    return {
        "x": jax.random.normal(ks[0], (BATCH, SEQ, D_MODEL), jnp.float32),
        "norm_mix_pre": gain(ks[1], D_MODEL),
        "w_in": jax.random.normal(ks[2], (L, D_MODEL, D_IN_PROJ), jnp.float32) * D_MODEL ** -0.5,
        "conv_w": jax.random.normal(ks[3], (L, CONV_WIDTH, D_XBC), jnp.float32) * CONV_WIDTH ** -0.5,
        "conv_b": 0.01 * jax.random.normal(ks[4], (L, D_XBC), jnp.float32),
        "dt_bias": dt_bias,
        "a_log": jnp.log(jax.random.uniform(ks[6], (L, SSM_HEADS), jnp.float32, 1.0, 16.0)),
        "d_skip": gain(ks[7], SSM_HEADS),
        "ssm_norm_w": gain(ks[8], D_SSM),
        "w_out": jax.random.normal(ks[9], (L, D_MIX, D_MODEL), jnp.float32) * D_MIX ** -0.5,
        "norm_mix_post": gain(ks[10], D_MODEL),
        "norm_mlp_pre": gain(ks[11], D_MODEL),
        "w_up": jax.random.normal(ks[12], (L, D_MODEL, D_FF), jnp.float32) * D_MODEL ** -0.5,
        "w_down": jax.random.normal(ks[13], (L, D_FF, D_MODEL), jnp.float32) * D_FF ** -0.5,
        "norm_mlp_post": gain(ks[14], D_MODEL),
    }


def reference(x, norm_mix_pre, w_in, conv_w, conv_b, dt_bias, a_log, d_skip, ssm_norm_w,
              w_out, norm_mix_post, norm_mlp_pre, w_up, w_down, norm_mlp_post):
    h = x
    for i in range(DEPTH):
        mix = hybrid_mixer(rmsnorm(h, norm_mix_pre[i]), w_in[i], conv_w[i], conv_b[i],
                           dt_bias[i], a_log[i], d_skip[i], ssm_norm_w[i], w_out[i])
        h = h + rmsnorm(mix, norm_mix_post[i])
        ff = squared_relu_mlp(rmsnorm(h, norm_mlp_pre[i]), w_up[i], w_down[i])
        h = h + rmsnorm(ff, norm_mlp_post[i])
    return h
```

```python
import functools

import numpy as np
import jax
import jax.numpy as jnp
from jax import lax
from jax.experimental import pallas as pl
from jax.experimental.pallas import tpu as pltpu

F32 = jnp.float32
BF16 = jnp.bfloat16

D_MODEL = 2048
SSM_HEAD_DIM = 64
SSM_HEADS = 32
SSM_GROUPS = 8
HEADS_PER_GROUP = 4
D_STATE = 128
D_SSM = SSM_HEADS * SSM_HEAD_DIM
D_BC = 2 * SSM_GROUPS * D_STATE
GROUP_WIDTH = HEADS_PER_GROUP * SSM_HEAD_DIM
CONV_WIDTH = 4
CHUNK = 128
ATT_HEADS = 16
ATT_HEAD_DIM = 128
D_ATT = ATT_HEADS * ATT_HEAD_DIM
D_MIX = D_SSM + D_ATT
D_FF = 4 * D_MODEL
WINDOW_REACH = 128
MAX_DILATION = 16
EPS = 1e-6
ATT_SCALE = ATT_HEAD_DIM ** -0.5
MASKED = -1e30

LANES = 128
VMEM_LIMIT = 56 * 1024 * 1024


def _params(semantics):
    return pltpu.CompilerParams(dimension_semantics=semantics, vmem_limit_bytes=VMEM_LIMIT)


def _silu(v):
    return v * (1.0 / (1.0 + jnp.exp(-v)))


def _rms_scale(v, gain):
    ms = jnp.mean(v * v, axis=-1, keepdims=True)
    return (v * lax.rsqrt(ms + EPS)) * gain


def _norm_rows(x_ref, g_ref, u_ref, rows, chunk=128):
    def body(c, carry):
        r0 = pl.multiple_of(c * chunk, chunk)
        u_ref[pl.ds(r0, chunk), :] = _rms_scale(x_ref[pl.ds(r0, chunk), :], g_ref[...]).astype(BF16)
        return carry
    lax.fori_loop(0, rows // chunk, body, 0)


IN_TM = 1024
IN_TN = 1024


def _inproj_ssm_kernel(x_ref, g_ref, w_ref, wdt_ref, o_ref, dt_ref, u_ref):
    @pl.when(pl.program_id(1) == 0)
    def _():
        _norm_rows(x_ref, g_ref, u_ref, IN_TM)
        dt_ref[...] = jnp.dot(u_ref[...], wdt_ref[...], preferred_element_type=F32)

    o_ref[...] = jnp.dot(u_ref[...], w_ref[...], preferred_element_type=F32)


def _inproj_att_kernel(x_ref, g_ref, w_ref, o_ref, u_ref, acc_ref):
    j = pl.program_id(1)

    @pl.when(j == 0)
    def _():
        _norm_rows(x_ref, g_ref, u_ref, IN_TM)

    res = jnp.dot(u_ref[...], w_ref[...], preferred_element_type=F32)
    for c in range(IN_TN // LANES):
        acc_ref[c] = res[:, c * LANES:(c + 1) * LANES]
    scale = jnp.where(j < D_ATT // IN_TN, ATT_SCALE, 1.0).astype(F32)
    rows = IN_TM // MAX_DILATION
    for r in range(MAX_DILATION):
        for c in range(IN_TN // LANES):
            o_ref[r, :, c * LANES:(c + 1) * LANES] = (
                acc_ref[c, pl.ds(r, rows, stride=MAX_DILATION), :] * scale).astype(BF16)


def _in_projection(x2, gain, w_ssm, w_dt, w_att, batch, seq):
    tokens = x2.shape[0]
    n_m = tokens // IN_TM
    m_per_seq = seq // IN_TM
    zxbc, dt_raw = pl.pallas_call(
        _inproj_ssm_kernel,
        grid=(n_m, w_ssm.shape[1] // IN_TN),
        in_specs=[
            pl.BlockSpec((IN_TM, D_MODEL), lambda i, j: (i, 0)),
            pl.BlockSpec((1, D_MODEL), lambda i, j: (0, 0)),
            pl.BlockSpec((D_MODEL, IN_TN), lambda i, j: (0, j)),
            pl.BlockSpec((D_MODEL, LANES), lambda i, j: (0, 0)),
        ],
        out_specs=[
            pl.BlockSpec((IN_TM, IN_TN), lambda i, j: (i, j)),
            pl.BlockSpec((IN_TM, LANES), lambda i, j: (i, 0)),
        ],
        out_shape=[
            jax.ShapeDtypeStruct((tokens, w_ssm.shape[1]), F32),
            jax.ShapeDtypeStruct((tokens, LANES), F32),
        ],
        scratch_shapes=[pltpu.VMEM((IN_TM, D_MODEL), BF16)],
        compiler_params=_params(("arbitrary", "arbitrary")),
        name="inproj_ssm",
    )(x2, gain, w_ssm, w_dt)

    run_len = seq // MAX_DILATION
    qkv = pl.pallas_call(
        _inproj_att_kernel,
        grid=(n_m, w_att.shape[1] // IN_TN),
        in_specs=[
            pl.BlockSpec((IN_TM, D_MODEL), lambda i, j: (i, 0)),
            pl.BlockSpec((1, D_MODEL), lambda i, j: (0, 0)),
            pl.BlockSpec((D_MODEL, IN_TN), lambda i, j: (0, j)),
        ],
        out_specs=pl.BlockSpec(
            (None, MAX_DILATION, IN_TM // MAX_DILATION, IN_TN),
            lambda i, j: (i // m_per_seq, 0, i % m_per_seq, j)),
        out_shape=jax.ShapeDtypeStruct((batch, MAX_DILATION, run_len, w_att.shape[1]), BF16),
        scratch_shapes=[pltpu.VMEM((IN_TM, D_MODEL), BF16),
                        pltpu.VMEM((IN_TN // LANES, IN_TM, LANES), F32)],
        compiler_params=_params(("arbitrary", "arbitrary")),
        name="inproj_att",
    )(x2, gain, w_att)
    return zxbc, dt_raw, qkv


CONV_PAD = 8


def _ssd_kernel(z_ref, x_ref, bc_ref, dt_ref, cwx_ref, cbx_ref, cwbc_ref, cbbc_ref,
                dtb_ref, alog_ref, dsk_ref, nw_ref, tri_ref, expand_ref,
                y_ref, state_ref, xpad_ref, bcpad_ref):
    @pl.when(pl.program_id(1) == 0)
    def _():
        state_ref[...] = jnp.zeros_like(state_ref)
        xpad_ref[0:CONV_PAD, :] = jnp.zeros((CONV_PAD, D_SSM), F32)
        bcpad_ref[0:CONV_PAD, :] = jnp.zeros((CONV_PAD, D_BC), F32)

    xpad_ref[CONV_PAD:CONV_PAD + CHUNK, :] = x_ref[...]
    bcpad_ref[CONV_PAD:CONV_PAD + CHUNK, :] = bc_ref[...]

    def conv_silu(pad_ref, w_ref, b_ref, c0, width):
        cols = slice(c0, c0 + width)
        acc = b_ref[:, cols]
        for k in range(CONV_WIDTH):
            start = CONV_PAD - (CONV_WIDTH - 1) + k
            acc = acc + w_ref[k:k + 1, cols] * pad_ref[start:start + CHUNK, cols]
        return _silu(acc)

    dt_in = dt_ref[...] + dtb_ref[...]
    dtv = jnp.maximum(dt_in, 0.0) + jnp.log1p(jnp.exp(-jnp.abs(dt_in)))
    da = dtv * (-jnp.exp(alog_ref[...]))
    a_cs = jnp.dot(tri_ref[...], da, precision=lax.Precision.HIGHEST,
                   preferred_element_type=F32)
    a_last = a_cs[CHUNK - 1:CHUNK, :]
    w_end = dtv * jnp.exp(a_last - a_cs)
    a_cs_t = a_cs.T
    dt_t = dtv.T
    w_end_t = w_end.T
    chunk_decay = jnp.dot(jnp.broadcast_to(jnp.exp(a_last), (8, LANES)), expand_ref[...],
                          precision=lax.Precision.HIGHEST,
                          preferred_element_type=F32)[0:1, :]

    row_i = lax.broadcasted_iota(jnp.int32, (CHUNK, CHUNK), 0)
    col_j = lax.broadcasted_iota(jnp.int32, (CHUNK, CHUNK), 1)
    causal = row_i >= col_j
    lane_head = lax.broadcasted_iota(jnp.int32, (CHUNK, GROUP_WIDTH), 1) // SSM_HEAD_DIM

    for g in range(SSM_GROUPS):
        gcols = slice(g * GROUP_WIDTH, (g + 1) * GROUP_WIDTH)
        xs = conv_silu(xpad_ref, cwx_ref, cbx_ref, g * GROUP_WIDTH, GROUP_WIDTH)
        bm = conv_silu(bcpad_ref, cwbc_ref, cbbc_ref, g * D_STATE, D_STATE)
        cm = conv_silu(bcpad_ref, cwbc_ref, cbbc_ref, D_BC // 2 + g * D_STATE, D_STATE)
        cb = lax.dot_general(cm.astype(BF16), bm.astype(BF16), (((1,), (1,)), ((), ())),
                             preferred_element_type=F32)
        bm_t = bm.T
        prev = state_ref[g]

        lhs_y, rhs_y, lhs_s, rhs_s = [], [], [], []
        for r in range(HEADS_PER_GROUP):
            h = g * HEADS_PER_GROUP + r
            col_a = jnp.broadcast_to(a_cs[:, h:h + 1], (CHUNK, CHUNK))
            row_a = a_cs_t[h:h + 1, :]
            decay = jnp.exp(jnp.where(causal, col_a - row_a, -jnp.inf))
            lhs_y.append((cb * decay * dt_t[h:h + 1, :]).astype(BF16))
            lhs_y.append((cm * jnp.exp(col_a)).astype(BF16))
            own = lane_head == r
            x_own = jnp.where(own, xs, 0.0).astype(BF16)
            rhs_y.append(x_own)
            rhs_y.append(jnp.where(own, prev, 0.0).astype(BF16))
            lhs_s.append((bm_t * w_end_t[h:h + 1, :]).astype(BF16))
            rhs_s.append(x_own)

        y = jnp.dot(jnp.concatenate(lhs_y, axis=1), jnp.concatenate(rhs_y, axis=0),
                    preferred_element_type=F32)
        s_new = jnp.dot(jnp.concatenate(lhs_s, axis=1), jnp.concatenate(rhs_s, axis=0),
                        preferred_element_type=F32)
        state_ref[g] = prev * chunk_decay[:, gcols] + s_new

        y = y + dsk_ref[:, gcols] * xs
        gated = y * _silu(z_ref[:, gcols])
        y_ref[:, gcols] = _rms_scale(gated, nw_ref[:, gcols]).astype(BF16)

    xpad_ref[0:CONV_PAD, :] = x_ref[CHUNK - CONV_PAD:CHUNK, :]
    bcpad_ref[0:CONV_PAD, :] = bc_ref[CHUNK - CONV_PAD:CHUNK, :]


def _ssd(zxbc, dt_raw, conv_w, conv_b, dt_bias, a_log, d_skip, ssm_norm_w, batch, seq):
    tokens = zxbc.shape[0]
    n_chunks = seq // CHUNK
    pad_heads = LANES - SSM_HEADS
    dtb = jnp.pad(dt_bias.astype(F32), (0, pad_heads)).reshape(1, LANES)
    alog = jnp.pad(a_log.astype(F32), (0, pad_heads)).reshape(1, LANES)
    dsk = jnp.repeat(d_skip.astype(F32), SSM_HEAD_DIM).reshape(1, D_SSM)
    tri = jnp.asarray(np.tril(np.ones((CHUNK, CHUNK), np.float32)))
    expand = np.zeros((LANES, D_SSM), np.float32)
    for h in range(SSM_HEADS):
        expand[h, h * SSM_HEAD_DIM:(h + 1) * SSM_HEAD_DIM] = 1.0
    expand = jnp.asarray(expand)

    def rows(b, c):
        return b * n_chunks + c

    def full(shape):
        return pl.BlockSpec(shape, lambda b, c: (0,) * len(shape))

    return pl.pallas_call(
        _ssd_kernel,
        grid=(batch, n_chunks),
        in_specs=[
            pl.BlockSpec((CHUNK, D_SSM), lambda b, c: (rows(b, c), 0)),
            pl.BlockSpec((CHUNK, D_SSM), lambda b, c: (rows(b, c), 1)),
            pl.BlockSpec((CHUNK, D_BC), lambda b, c: (rows(b, c), 2)),
            pl.BlockSpec((CHUNK, LANES), lambda b, c: (rows(b, c), 0)),
            full((CONV_WIDTH, D_SSM)), full((1, D_SSM)),
            full((CONV_WIDTH, D_BC)), full((1, D_BC)),
            full((1, LANES)), full((1, LANES)), full((1, D_SSM)), full((1, D_SSM)),
            full((CHUNK, CHUNK)), full((LANES, D_SSM)),
        ],
        out_specs=pl.BlockSpec((CHUNK, D_SSM), lambda b, c: (rows(b, c), 0)),
        out_shape=jax.ShapeDtypeStruct((tokens, D_SSM), BF16),
        scratch_shapes=[
            pltpu.VMEM((SSM_GROUPS, D_STATE, GROUP_WIDTH), F32),
            pltpu.VMEM((CONV_PAD + CHUNK, D_SSM), F32),
            pltpu.VMEM((CONV_PAD + CHUNK, D_BC), F32),
        ],
        compiler_params=_params(("arbitrary", "arbitrary")),
        name="ssd",
    )(zxbc, zxbc, zxbc, dt_raw,
      conv_w[:, :D_SSM], conv_b[:D_SSM].reshape(1, D_SSM),
      conv_w[:, D_SSM:], conv_b[D_SSM:].reshape(1, D_BC),
      dtb, alog, dsk, ssm_norm_w.reshape(1, D_SSM), tri, expand)


BRANCHES = ((16, 16, 16), (4, 32, 32), (1, 128, 128))


def _branch_bias(n_runs, width, back):
    s_q = np.repeat(np.arange(n_runs), width)
    i_q = np.tile(np.arange(width), n_runs)
    j_q = n_runs * i_q + s_q
    s_p = np.repeat(np.arange(n_runs), back)
    i_p = np.tile(np.arange(back), n_runs) - back
    j_k = np.concatenate([n_runs * i_p + s_p, j_q])
    dist = j_q[:, None] - j_k[None, :]
    return np.where((dist >= 0) & (dist <= WINDOW_REACH), 0.0, MASKED).astype(np.float32)


def _attn_kernel(q_ref, k_ref, v_ref, b0_ref, b1_ref, b2_ref, o_ref, num_ref, m_ref, l_ref, *, run_len):
    bias_refs = (b0_ref, b1_ref, b2_ref)

    def block(branch, cls, i0, with_prev):
        n_runs, width, back = BRANCHES[branch]
        n_cls = MAX_DILATION // n_runs
        starts = [pl.multiple_of((cls + n_cls * s) * run_len + i0, width) for s in range(n_runs)]
        q = jnp.concatenate([q_ref[pl.ds(st, width), :] for st in starts], axis=0)
        k_parts = [k_ref[pl.ds(st, width), :] for st in starts]
        v_parts = [v_ref[pl.ds(st, width), :] for st in starts]
        if with_prev:
            prevs = [pl.multiple_of(st - back, back) for st in starts]
            k_parts = [k_ref[pl.ds(st, back), :] for st in prevs] + k_parts
            v_parts = [v_ref[pl.ds(st, back), :] for st in prevs] + v_parts
            bias = bias_refs[branch][...]
        else:
            bias = bias_refs[branch][:, n_runs * back:]
        k = jnp.concatenate(k_parts, axis=0)
        v = jnp.concatenate(v_parts, axis=0)
        s = lax.dot_general(q, k, (((1,), (1,)), ((), ())), preferred_element_type=F32) + bias
        m = jnp.max(s, axis=-1, keepdims=True)
        p = jnp.exp(s - m)
        l = jnp.sum(p, axis=-1, keepdims=True)
        num = jnp.dot(p.astype(BF16), v, preferred_element_type=F32)
        for idx, st in enumerate(starts):
            part = slice(idx * width, (idx + 1) * width)
            num_ref[branch, pl.ds(st, width), :] = num[part]
            m_ref[branch, pl.ds(st, width), :] = m[part]
            l_ref[branch, pl.ds(st, width), :] = l[part]

    for branch, (n_runs, width, back) in enumerate(BRANCHES):
        n_cls = MAX_DILATION // n_runs
        later = run_len // width - 1

        def first(cls, carry, branch=branch):
            block(branch, cls, 0, False)
            return carry
        lax.fori_loop(0, n_cls, first, 0)

        def rest(t, carry, branch=branch, width=width, later=later):
            block(branch, t // later, (t % later + 1) * width, True)
            return carry
        if later:
            lax.fori_loop(0, n_cls * later, rest, 0)

    rows = 256

    def combine(c, carry):
        sl = pl.ds(pl.multiple_of(c * rows, rows), rows)
        m0, m1, m2 = m_ref[0, sl, :], m_ref[1, sl, :], m_ref[2, sl, :]
        top = jnp.maximum(jnp.maximum(m0, m1), m2)
        w0, w1, w2 = jnp.exp(m0 - top), jnp.exp(m1 - top), jnp.exp(m2 - top)
        den = w0 * l_ref[0, sl, :] + w1 * l_ref[1, sl, :] + w2 * l_ref[2, sl, :]
        num = w0 * num_ref[0, sl, :] + w1 * num_ref[1, sl, :] + w2 * num_ref[2, sl, :]
        o_ref[sl, :] = (num / den).astype(BF16)
        return carry
    lax.fori_loop(0, (MAX_DILATION * run_len) // rows, combine, 0)


def _attention(qkv, batch, seq):
    run_len = seq // MAX_DILATION
    qkv2 = qkv.reshape(batch, seq, 3 * D_ATT)
    biases = [jnp.asarray(_branch_bias(*b)) for b in BRANCHES]

    def head_block(offset):
        return pl.BlockSpec((None, seq, ATT_HEAD_DIM), lambda b, h: (b, 0, offset + h))

    def full(shape):
        return pl.BlockSpec(shape, lambda b, h: (0,) * len(shape))

    return pl.pallas_call(
        functools.partial(_attn_kernel, run_len=run_len),
        grid=(batch, ATT_HEADS),
        in_specs=[head_block(0), head_block(ATT_HEADS), head_block(2 * ATT_HEADS)]
        + [full(b.shape) for b in biases],
        out_specs=pl.BlockSpec((None, seq, ATT_HEAD_DIM), lambda b, h: (b, 0, h)),
        out_shape=jax.ShapeDtypeStruct((batch, seq, D_ATT), BF16),
        scratch_shapes=[
            pltpu.VMEM((len(BRANCHES), seq, ATT_HEAD_DIM), F32),
            pltpu.VMEM((len(BRANCHES), seq, 1), F32),
            pltpu.VMEM((len(BRANCHES), seq, 1), F32),
        ],
        compiler_params=_params(("arbitrary", "arbitrary")),
        name="dilated_attn",
    )(qkv2, qkv2, qkv2, *biases)


OUT_TM = 512
OUT_TK = 1024


def _outproj_kernel(ys_ref, ya_ref, w_ref, x_ref, gpost_ref, gpre_ref, h_ref, u_ref, accs_ref, acca_ref):
    k = pl.program_id(1)
    n_ssm = D_SSM // OUT_TK

    @pl.when(k == 0)
    def _():
        accs_ref[...] = jnp.dot(ys_ref[...], w_ref[...], preferred_element_type=F32)

    @pl.when((k > 0) & (k < n_ssm))
    def _():
        accs_ref[...] += jnp.dot(ys_ref[...], w_ref[...], preferred_element_type=F32)

    n_slabs = D_MODEL // LANES

    def att_part():
        return jnp.dot(ya_ref[...].reshape(OUT_TM, OUT_TK), w_ref[...], preferred_element_type=F32)

    @pl.when(k == n_ssm)
    def _():
        res = att_part()
        for c in range(n_slabs):
            acca_ref[c] = res[:, c * LANES:(c + 1) * LANES]

    @pl.when(k > n_ssm)
    def _():
        res = att_part()
        for c in range(n_slabs):
            acca_ref[c] += res[:, c * LANES:(c + 1) * LANES]

    @pl.when(k == pl.num_programs(1) - 1)
    def _():
        rows = OUT_TM // MAX_DILATION

        def body(i, carry):
            nat = pl.ds(pl.multiple_of(i * MAX_DILATION, MAX_DILATION), MAX_DILATION)
            att = jnp.concatenate(
                [acca_ref[c, pl.ds(i, MAX_DILATION, stride=rows), :] for c in range(n_slabs)], axis=1)
            mix = accs_ref[nat, :] + att
            h = x_ref[nat, :] + _rms_scale(mix, gpost_ref[...])
            h_ref[nat, :] = h
            u_ref[nat, :] = _rms_scale(h, gpre_ref[...]).astype(BF16)
            return carry
        lax.fori_loop(0, rows, body, 0)


def _out_projection(y_ssm, y_att, w_out, x2, g_post, g_pre, batch, seq):
    tokens = x2.shape[0]
    n_m = tokens // OUT_TM
    m_per_seq = seq // OUT_TM
    n_ssm = D_SSM // OUT_TK
    run_len = seq // MAX_DILATION
    y_att4 = y_att.reshape(batch, MAX_DILATION, run_len, D_ATT)
    return pl.pallas_call(
        _outproj_kernel,
        grid=(n_m, D_MIX // OUT_TK),
        in_specs=[
            pl.BlockSpec((OUT_TM, OUT_TK), lambda i, k: (i, jnp.minimum(k, n_ssm - 1))),
            pl.BlockSpec((None, MAX_DILATION, OUT_TM // MAX_DILATION, OUT_TK),
                         lambda i, k: (i // m_per_seq, 0, i % m_per_seq, jnp.maximum(k - n_ssm, 0))),
            pl.BlockSpec((OUT_TK, D_MODEL), lambda i, k: (k, 0)),
            pl.BlockSpec((OUT_TM, D_MODEL), lambda i, k: (i, 0)),
            pl.BlockSpec((1, D_MODEL), lambda i, k: (0, 0)),
            pl.BlockSpec((1, D_MODEL), lambda i, k: (0, 0)),
        ],
        out_specs=[
            pl.BlockSpec((OUT_TM, D_MODEL), lambda i, k: (i, 0)),
            pl.BlockSpec((OUT_TM, D_MODEL), lambda i, k: (i, 0)),
        ],
        out_shape=[
            jax.ShapeDtypeStruct((tokens, D_MODEL), F32),
            jax.ShapeDtypeStruct((tokens, D_MODEL), BF16),
        ],
        scratch_shapes=[pltpu.VMEM((OUT_TM, D_MODEL), F32),
                        pltpu.VMEM((D_MODEL // LANES, OUT_TM, LANES), F32)],
        compiler_params=_params(("arbitrary", "arbitrary")),
        name="outproj",
    )(y_ssm, y_att4, w_out, x2, g_post, g_pre)


MLP_TM = 512
MLP_TF = 1024


def _mlp_kernel(u_ref, wup_ref, wdown_ref, h_ref, g_ref, o_ref, acc_ref):
    f = pl.program_id(1)
    hid = jnp.maximum(jnp.dot(u_ref[...], wup_ref[...], preferred_element_type=F32), 0.0)
    part = jnp.dot((hid * hid).astype(BF16), wdown_ref[...], preferred_element_type=F32)

    @pl.when(f == 0)
    def _():
        acc_ref[...] = part

    @pl.when(f > 0)
    def _():
        acc_ref[...] += part

    @pl.when(f == pl.num_programs(1) - 1)
    def _():
        chunk = 128

        def body(c, carry):
            sl = pl.ds(pl.multiple_of(c * chunk, chunk), chunk)
            o_ref[sl, :] = h_ref[sl, :] + _rms_scale(acc_ref[sl, :], g_ref[...])
            return carry
        lax.fori_loop(0, MLP_TM // chunk, body, 0)


def _mlp(u2, w_up, w_down, h1, g_post):
    tokens = u2.shape[0]
    return pl.pallas_call(
        _mlp_kernel,
        grid=(tokens // MLP_TM, D_FF // MLP_TF),
        in_specs=[
            pl.BlockSpec((MLP_TM, D_MODEL), lambda i, f: (i, 0)),
            pl.BlockSpec((D_MODEL, MLP_TF), lambda i, f: (0, f)),
            pl.BlockSpec((MLP_TF, D_MODEL), lambda i, f: (f, 0)),
            pl.BlockSpec((MLP_TM, D_MODEL), lambda i, f: (i, 0)),
            pl.BlockSpec((1, D_MODEL), lambda i, f: (0, 0)),
        ],
        out_specs=pl.BlockSpec((MLP_TM, D_MODEL), lambda i, f: (i, 0)),
        out_shape=jax.ShapeDtypeStruct((tokens, D_MODEL), F32),
        scratch_shapes=[pltpu.VMEM((MLP_TM, D_MODEL), F32)],
        compiler_params=_params(("arbitrary", "arbitrary")),
        name="mlp",
    )(u2, w_up, w_down, h1, g_post)


def kernel(x, norm_mix_pre, w_in, conv_w, conv_b, dt_bias, a_log, d_skip, ssm_norm_w, w_out,
           norm_mix_post, norm_mlp_pre, w_up, w_down, norm_mlp_post):
    batch, seq, _ = x.shape
    depth = w_in.shape[0]
    ssm_cols = D_SSM + D_SSM + D_BC
    h = x.reshape(batch * seq, D_MODEL)
    for layer in range(depth):
        w = w_in[layer]
        w_ssm = w[:, :ssm_cols].astype(BF16)
        w_dt = jnp.pad(w[:, ssm_cols:ssm_cols + SSM_HEADS], ((0, 0), (0, LANES - SSM_HEADS))).astype(BF16)
        w_att = w[:, ssm_cols + SSM_HEADS:].astype(BF16)
        zxbc, dt_raw, qkv = _in_projection(
            h, norm_mix_pre[layer].reshape(1, D_MODEL), w_ssm, w_dt, w_att, batch, seq)
        y_ssm = _ssd(zxbc, dt_raw, conv_w[layer], conv_b[layer], dt_bias[layer], a_log[layer],
                     d_skip[layer], ssm_norm_w[layer], batch, seq)
        y_att = _attention(qkv, batch, seq)
        h1, u2 = _out_projection(
            y_ssm, y_att, w_out[layer].astype(BF16), h,
            norm_mix_post[layer].reshape(1, D_MODEL), norm_mlp_pre[layer].reshape(1, D_MODEL), batch, seq)
        h = _mlp(u2, w_up[layer].astype(BF16), w_down[layer].astype(BF16), h1,
                 norm_mlp_post[layer].reshape(1, D_MODEL))
    return h.reshape(batch, seq, D_MODEL)
```

```python
import functools

import numpy as np
import jax
import jax.numpy as jnp
from jax import lax
from jax.experimental import pallas as pl
from jax.experimental.pallas import tpu as pltpu

F32 = jnp.float32
BF16 = jnp.bfloat16

D_MODEL = 2048
SSM_HEAD_DIM = 64
SSM_HEADS = 32
SSM_GROUPS = 8
HEADS_PER_GROUP = 4
D_STATE = 128
D_SSM = SSM_HEADS * SSM_HEAD_DIM
D_BC = 2 * SSM_GROUPS * D_STATE
GROUP_WIDTH = HEADS_PER_GROUP * SSM_HEAD_DIM
CONV_WIDTH = 4
CHUNK = 128
ATT_HEADS = 16
ATT_HEAD_DIM = 128
D_ATT = ATT_HEADS * ATT_HEAD_DIM
D_MIX = D_SSM + D_ATT
D_FF = 4 * D_MODEL
WINDOW_REACH = 128
MAX_DILATION = 16
EPS = 1e-6
ATT_SCALE = ATT_HEAD_DIM ** -0.5
MASKED = -1e30

LANES = 128
VMEM_LIMIT = 56 * 1024 * 1024


def _params(semantics):
    return pltpu.CompilerParams(dimension_semantics=semantics, vmem_limit_bytes=VMEM_LIMIT)


def _silu(v):
    return v * (1.0 / (1.0 + jnp.exp(-v)))


def _rms_scale(v, gain):
    ms = jnp.mean(v * v, axis=-1, keepdims=True)
    return (v * lax.rsqrt(ms + EPS)) * gain


IN_TM = 1024
IN_TN = 1024


def _inproj_kernel(x_ref, g_ref, w_ref, wdt_ref, zxbc_ref, dt_ref, qkv_ref, u_ref, uperm_ref, uf_ref,
                   *, n_ssm_tiles):
    j = pl.program_id(1)
    chunk = 128
    n_slabs = D_MODEL // LANES
    rows = IN_TM // MAX_DILATION

    @pl.when(j == 0)
    def _():
        def body(c, carry):
            sl = pl.ds(pl.multiple_of(c * chunk, chunk), chunk)
            u = _rms_scale(x_ref[sl, :], g_ref[...])
            u_ref[sl, :] = u.astype(BF16)
            for s in range(n_slabs):
                uf_ref[s, sl, :] = u[:, s * LANES:(s + 1) * LANES]
            return carry
        lax.fori_loop(0, IN_TM // chunk, body, 0)
        dt_ref[...] = jnp.dot(u_ref[...], wdt_ref[...], preferred_element_type=F32)
        for r in range(MAX_DILATION):
            for s in range(n_slabs):
                uperm_ref[r * rows:(r + 1) * rows, s * LANES:(s + 1) * LANES] = (
                    uf_ref[s, pl.ds(r, rows, stride=MAX_DILATION), :].astype(BF16))

    @pl.when(j < n_ssm_tiles)
    def _():
        zxbc_ref[...] = jnp.dot(u_ref[...], w_ref[...], preferred_element_type=F32)

    @pl.when(j >= n_ssm_tiles)
    def _():
        scale = jnp.where(j < n_ssm_tiles + D_ATT // IN_TN, ATT_SCALE, 1.0).astype(F32)
        res = jnp.dot(uperm_ref[...], w_ref[...], preferred_element_type=F32) * scale
        for r in range(MAX_DILATION):
            qkv_ref[r] = res[r * rows:(r + 1) * rows, :].astype(BF16)


def _in_projection(x2, gain, w_main, w_dt, batch, seq):
    tokens = x2.shape[0]
    n_m = tokens // IN_TM
    m_per_seq = seq // IN_TM
    ssm_cols = 2 * D_SSM + D_BC
    att_cols = 3 * D_ATT
    n_ssm_tiles = ssm_cols // IN_TN
    n_att_tiles = att_cols // IN_TN
    run_len = seq // MAX_DILATION
    return pl.pallas_call(
        functools.partial(_inproj_kernel, n_ssm_tiles=n_ssm_tiles),
        grid=(n_m, n_ssm_tiles + n_att_tiles),
        in_specs=[
            pl.BlockSpec((IN_TM, D_MODEL), lambda i, j: (i, 0), pipeline_mode=pl.Buffered(1)),
            pl.BlockSpec((1, D_MODEL), lambda i, j: (0, 0)),
            pl.BlockSpec((D_MODEL, IN_TN), lambda i, j: (0, j)),
            pl.BlockSpec((D_MODEL, LANES), lambda i, j: (0, 0), pipeline_mode=pl.Buffered(1)),
        ],
        out_specs=[
            pl.BlockSpec((IN_TM, IN_TN), lambda i, j: (i, jnp.minimum(j, n_ssm_tiles - 1))),
            pl.BlockSpec((IN_TM, LANES), lambda i, j: (i, 0)),
            pl.BlockSpec((None, MAX_DILATION, IN_TM // MAX_DILATION, IN_TN),
                         lambda i, j: (i // m_per_seq, 0, i % m_per_seq, jnp.maximum(j - n_ssm_tiles, 0))),
        ],
        out_shape=[
            jax.ShapeDtypeStruct((tokens, ssm_cols), F32),
            jax.ShapeDtypeStruct((tokens, LANES), F32),
            jax.ShapeDtypeStruct((batch, MAX_DILATION, run_len, att_cols), BF16),
        ],
        scratch_shapes=[
            pltpu.VMEM((IN_TM, D_MODEL), BF16),
            pltpu.VMEM((IN_TM, D_MODEL), BF16),
            pltpu.VMEM((D_MODEL // LANES, IN_TM, LANES), F32),
        ],
        compiler_params=_params(("arbitrary", "arbitrary")),
        name="inproj",
    )(x2, gain, w_main, w_dt)


CONV_PAD = 8


def _ssd_kernel(z_ref, x_ref, bc_ref, dt_ref, cwx_ref, cbx_ref, cwbc_ref, cbbc_ref,
                dtb_ref, alog_ref, dsk_ref, nw_ref, tri_ref, expand_ref,
                y_ref, state_ref, xpad_ref, bcpad_ref):
    @pl.when(pl.program_id(1) == 0)
    def _():
        state_ref[...] = jnp.zeros_like(state_ref)
        xpad_ref[0:CONV_PAD, :] = jnp.zeros((CONV_PAD, D_SSM), F32)
        bcpad_ref[0:CONV_PAD, :] = jnp.zeros((CONV_PAD, D_BC), F32)

    xpad_ref[CONV_PAD:CONV_PAD + CHUNK, :] = x_ref[...]
    bcpad_ref[CONV_PAD:CONV_PAD + CHUNK, :] = bc_ref[...]

    def conv_silu(pad_ref, w_ref, b_ref, c0, width):
        cols = slice(c0, c0 + width)
        acc = b_ref[:, cols]
        for k in range(CONV_WIDTH):
            start = CONV_PAD - (CONV_WIDTH - 1) + k
            acc = acc + w_ref[k:k + 1, cols] * pad_ref[start:start + CHUNK, cols]
        return _silu(acc)

    dt_in = dt_ref[...] + dtb_ref[...]
    dtv = jnp.maximum(dt_in, 0.0) + jnp.log1p(jnp.exp(-jnp.abs(dt_in)))
    da = dtv * (-jnp.exp(alog_ref[...]))
    a_cs = jnp.dot(tri_ref[...], da, precision=lax.Precision.HIGHEST,
                   preferred_element_type=F32)
    a_last = a_cs[CHUNK - 1:CHUNK, :]
    w_end = dtv * jnp.exp(a_last - a_cs)
    a_cs_t = a_cs.T
    dt_t = dtv.T
    w_end_t = w_end.T
    chunk_decay = jnp.dot(jnp.broadcast_to(jnp.exp(a_last), (8, LANES)), expand_ref[...],
                          precision=lax.Precision.HIGHEST,
                          preferred_element_type=F32)[0:1, :]

    row_i = lax.broadcasted_iota(jnp.int32, (CHUNK, CHUNK), 0)
    col_j = lax.broadcasted_iota(jnp.int32, (CHUNK, CHUNK), 1)
    causal = row_i >= col_j
    lane_head = lax.broadcasted_iota(jnp.int32, (CHUNK, GROUP_WIDTH), 1) // SSM_HEAD_DIM

    for g in range(SSM_GROUPS):
        gcols = slice(g * GROUP_WIDTH, (g + 1) * GROUP_WIDTH)
        xs = conv_silu(xpad_ref, cwx_ref, cbx_ref, g * GROUP_WIDTH, GROUP_WIDTH)
        bm = conv_silu(bcpad_ref, cwbc_ref, cbbc_ref, g * D_STATE, D_STATE)
        cm = conv_silu(bcpad_ref, cwbc_ref, cbbc_ref, D_BC // 2 + g * D_STATE, D_STATE)
        cb = lax.dot_general(cm.astype(BF16), bm.astype(BF16), (((1,), (1,)), ((), ())),
                             preferred_element_type=F32)
        bm_t = bm.T
        prev = state_ref[g]

        lhs_y, rhs_y, lhs_s, rhs_s = [], [], [], []
        for r in range(HEADS_PER_GROUP):
            h = g * HEADS_PER_GROUP + r
            col_a = jnp.broadcast_to(a_cs[:, h:h + 1], (CHUNK, CHUNK))
            row_a = a_cs_t[h:h + 1, :]
            decay = jnp.exp(jnp.where(causal, col_a - row_a, -jnp.inf))
            lhs_y.append((cb * decay * dt_t[h:h + 1, :]).astype(BF16))
            lhs_y.append((cm * jnp.exp(col_a)).astype(BF16))
            own = lane_head == r
            x_own = jnp.where(own, xs, 0.0).astype(BF16)
            rhs_y.append(x_own)
            rhs_y.append(jnp.where(own, prev, 0.0).astype(BF16))
            lhs_s.append((bm_t * w_end_t[h:h + 1, :]).astype(BF16))
            rhs_s.append(x_own)

        y = jnp.dot(jnp.concatenate(lhs_y, axis=1), jnp.concatenate(rhs_y, axis=0),
                    preferred_element_type=F32)
        s_new = jnp.dot(jnp.concatenate(lhs_s, axis=1), jnp.concatenate(rhs_s, axis=0),
                        preferred_element_type=F32)
        state_ref[g] = prev * chunk_decay[:, gcols] + s_new

        y = y + dsk_ref[:, gcols] * xs
        gated = y * _silu(z_ref[:, gcols])
        y_ref[:, gcols] = _rms_scale(gated, nw_ref[:, gcols]).astype(BF16)

    xpad_ref[0:CONV_PAD, :] = x_ref[CHUNK - CONV_PAD:CHUNK, :]
    bcpad_ref[0:CONV_PAD, :] = bc_ref[CHUNK - CONV_PAD:CHUNK, :]


def _ssd(zxbc, dt_raw, conv_w, conv_b, dt_bias, a_log, d_skip, ssm_norm_w, batch, seq):
    tokens = zxbc.shape[0]
    n_chunks = seq // CHUNK
    pad_heads = LANES - SSM_HEADS
    dtb = jnp.pad(dt_bias.astype(F32), (0, pad_heads)).reshape(1, LANES)
    alog = jnp.pad(a_log.astype(F32), (0, pad_heads)).reshape(1, LANES)
    dsk = jnp.repeat(d_skip.astype(F32), SSM_HEAD_DIM).reshape(1, D_SSM)
    tri = jnp.asarray(np.tril(np.ones((CHUNK, CHUNK), np.float32)))
    expand = np.zeros((LANES, D_SSM), np.float32)
    for h in range(SSM_HEADS):
        expand[h, h * SSM_HEAD_DIM:(h + 1) * SSM_HEAD_DIM] = 1.0
    expand = jnp.asarray(expand)

    def rows(b, c):
        return b * n_chunks + c

    def full(shape):
        return pl.BlockSpec(shape, lambda b, c: (0,) * len(shape))

    return pl.pallas_call(
        _ssd_kernel,
        grid=(batch, n_chunks),
        in_specs=[
            pl.BlockSpec((CHUNK, D_SSM), lambda b, c: (rows(b, c), 0)),
            pl.BlockSpec((CHUNK, D_SSM), lambda b, c: (rows(b, c), 1)),
            pl.BlockSpec((CHUNK, D_BC), lambda b, c: (rows(b, c), 2)),
            pl.BlockSpec((CHUNK, LANES), lambda b, c: (rows(b, c), 0)),
            full((CONV_WIDTH, D_SSM)), full((1, D_SSM)),
            full((CONV_WIDTH, D_BC)), full((1, D_BC)),
            full((1, LANES)), full((1, LANES)), full((1, D_SSM)), full((1, D_SSM)),
            full((CHUNK, CHUNK)), full((LANES, D_SSM)),
        ],
        out_specs=pl.BlockSpec((CHUNK, D_SSM), lambda b, c: (rows(b, c), 0)),
        out_shape=jax.ShapeDtypeStruct((tokens, D_SSM), BF16),
        scratch_shapes=[
            pltpu.VMEM((SSM_GROUPS, D_STATE, GROUP_WIDTH), F32),
            pltpu.VMEM((CONV_PAD + CHUNK, D_SSM), F32),
            pltpu.VMEM((CONV_PAD + CHUNK, D_BC), F32),
        ],
        compiler_params=_params(("arbitrary", "arbitrary")),
        name="ssd",
    )(zxbc, zxbc, zxbc, dt_raw,
      conv_w[:, :D_SSM], conv_b[:D_SSM].reshape(1, D_SSM),
      conv_w[:, D_SSM:], conv_b[D_SSM:].reshape(1, D_BC),
      dtb, alog, dsk, ssm_norm_w.reshape(1, D_SSM), tri, expand)


BRANCHES = ((16, 16, 16), (4, 32, 32), (1, 128, 128))
ATT_GROUP = (4, 8, 8)


def _branch_bias(n_runs, width, back):
    s_q = np.repeat(np.arange(n_runs), width)
    i_q = np.tile(np.arange(width), n_runs)
    j_q = n_runs * i_q + s_q
    s_p = np.repeat(np.arange(n_runs), back)
    i_p = np.tile(np.arange(back), n_runs) - back
    j_k = np.concatenate([n_runs * i_p + s_p, j_q])
    dist = j_q[:, None] - j_k[None, :]
    return np.where((dist >= 0) & (dist <= WINDOW_REACH), 0.0, MASKED).astype(np.float32)


def _attn_kernel(q_ref, k_ref, v_ref, b0_ref, b1_ref, b2_ref, o_ref,
                 num_ref, m_ref, l_ref, nat_ref, s0_ref, p0_ref, s1_ref, p1_ref, *, run_len):
    bias_refs = (b0_ref, b1_ref, b2_ref)
    stage_refs = ((s0_ref, p0_ref), (s1_ref, p1_ref), (s1_ref, p1_ref))

    def rows_of(ref, starts, size):
        parts = [ref[pl.ds(st, size), :] for st in starts]
        return parts[0] if len(parts) == 1 else jnp.concatenate(parts, axis=0)

    def run_group(branch, blocks):
        n_runs, width, back = BRANCHES[branch]
        n_cls = MAX_DILATION // n_runs
        s_ref, p_ref = stage_refs[branch]
        plan = []
        for u, (cls, i0, with_prev) in enumerate(blocks):
            starts = [pl.multiple_of((cls + n_cls * s) * run_len + i0, width) for s in range(n_runs)]
            prevs = [pl.multiple_of(st - back, back) for st in starts] if with_prev else []
            n_keys = n_runs * (width + (back if with_prev else 0))
            plan.append((u, starts, prevs, n_keys))

        def keys_of(ref, starts, prevs):
            cur = rows_of(ref, starts, width)
            return jnp.concatenate([rows_of(ref, prevs, back), cur], axis=0) if prevs else cur

        for u, starts, prevs, n_keys in plan:
            q = rows_of(q_ref, starts, width)
            k = keys_of(k_ref, starts, prevs)
            bias = bias_refs[branch][...] if prevs else bias_refs[branch][:, n_runs * back:]
            s_ref[u, :, 0:n_keys] = lax.dot_general(
                q, k, (((1,), (1,)), ((), ())), preferred_element_type=F32) + bias
        for u, starts, prevs, n_keys in plan:
            s = s_ref[u, :, 0:n_keys]
            m = jnp.max(s, axis=-1, keepdims=True)
            p_ref[u, :, 0:n_keys] = jnp.exp(s - m).astype(BF16)
            for idx, st in enumerate(starts):
                m_ref[branch, pl.ds(st, width), :] = jnp.broadcast_to(
                    m[idx * width:(idx + 1) * width], (width, ATT_HEAD_DIM))
        for u, starts, prevs, n_keys in plan:
            v = keys_of(v_ref, starts, prevs)
            v1 = jnp.concatenate([v, jnp.ones((n_keys, ATT_HEAD_DIM), BF16)], axis=1)
            pv = jnp.dot(p_ref[u, :, 0:n_keys], v1, preferred_element_type=F32)
            for idx, st in enumerate(starts):
                part = slice(idx * width, (idx + 1) * width)
                num_ref[branch, pl.ds(st, width), :] = pv[part, 0:ATT_HEAD_DIM]
                l_ref[branch, pl.ds(st, width), :] = pv[part, ATT_HEAD_DIM:]

    for branch, (n_runs, width, back) in enumerate(BRANCHES):
        n_cls = MAX_DILATION // n_runs
        per_cls = run_len // width
        group = ATT_GROUP[branch]
        if group >= per_cls:
            cls_per_iter = group // per_cls

            def whole(t, carry, branch=branch, width=width, per_cls=per_cls, cls_per_iter=cls_per_iter):
                run_group(branch, [(t * cls_per_iter + u, blk * width, blk > 0)
                                   for u in range(cls_per_iter) for blk in range(per_cls)])
                return carry
            lax.fori_loop(0, n_cls // cls_per_iter, whole, 0)
        else:
            groups = per_cls // group

            def head(cls, carry, branch=branch, width=width, group=group):
                run_group(branch, [(cls, u * width, u > 0) for u in range(group)])
                return carry
            lax.fori_loop(0, n_cls, head, 0)

            def tail(t, carry, branch=branch, width=width, group=group, groups=groups):
                cls, sub = t // (groups - 1), t % (groups - 1) + 1
                run_group(branch, [(cls, (sub * group + u) * width, True) for u in range(group)])
                return carry
            if groups > 1:
                lax.fori_loop(0, n_cls * (groups - 1), tail, 0)

    def combine(r, carry):
        sl = pl.ds(pl.multiple_of(r * run_len, run_len), run_len)
        m0, m1, m2 = m_ref[0, sl, :], m_ref[1, sl, :], m_ref[2, sl, :]
        top = jnp.maximum(jnp.maximum(m0, m1), m2)
        w0, w1, w2 = jnp.exp(m0 - top), jnp.exp(m1 - top), jnp.exp(m2 - top)
        den = w0 * l_ref[0, sl, :] + w1 * l_ref[1, sl, :] + w2 * l_ref[2, sl, :]
        num = w0 * num_ref[0, sl, :] + w1 * num_ref[1, sl, :] + w2 * num_ref[2, sl, :]
        nat_ref[pl.ds(r, run_len, stride=MAX_DILATION), :] = num / den
        return carry
    lax.fori_loop(0, MAX_DILATION, combine, 0)

    rows = 256

    def emit(c, carry):
        sl = pl.ds(pl.multiple_of(c * rows, rows), rows)
        o_ref[sl, :] = nat_ref[sl, :].astype(BF16)
        return carry
    lax.fori_loop(0, (MAX_DILATION * run_len) // rows, emit, 0)


def _stage_scratch():
    out = []
    for branch in (0, 1):
        n_runs, width, back = BRANCHES[branch]
        shape = (ATT_GROUP[branch], n_runs * width, n_runs * (width + back))
        out += [pltpu.VMEM(shape, F32), pltpu.VMEM(shape, BF16)]
    return out


def _attention(qkv, batch, seq):
    run_len = seq // MAX_DILATION
    qkv2 = qkv.reshape(batch, seq, 3 * D_ATT)
    biases = [jnp.asarray(_branch_bias(*b)) for b in BRANCHES]

    def head_block(offset):
        return pl.BlockSpec((None, seq, ATT_HEAD_DIM), lambda b, h: (b, 0, offset + h))

    def full(shape):
        return pl.BlockSpec(shape, lambda b, h: (0,) * len(shape))

    return pl.pallas_call(
        functools.partial(_attn_kernel, run_len=run_len),
        grid=(batch, ATT_HEADS),
        in_specs=[head_block(0), head_block(ATT_HEADS), head_block(2 * ATT_HEADS)]
        + [full(b.shape) for b in biases],
        out_specs=pl.BlockSpec((None, seq, ATT_HEAD_DIM), lambda b, h: (b, 0, h)),
        out_shape=jax.ShapeDtypeStruct((batch, seq, D_ATT), BF16),
        scratch_shapes=[
            pltpu.VMEM((len(BRANCHES), seq, ATT_HEAD_DIM), F32),
            pltpu.VMEM((len(BRANCHES), seq, ATT_HEAD_DIM), F32),
            pltpu.VMEM((len(BRANCHES), seq, ATT_HEAD_DIM), F32),
            pltpu.VMEM((seq, ATT_HEAD_DIM), F32),
        ] + _stage_scratch(),
        compiler_params=_params(("arbitrary", "arbitrary")),
        name="dilated_attn",
    )(qkv2, qkv2, qkv2, *biases)


OUT_TM = 512
OUT_TK = 1024


def _outproj_kernel(ys_ref, ya_ref, w_ref, x_ref, gpost_ref, gpre_ref, h_ref, u_ref, acc_ref):
    k = pl.program_id(1)
    n_ssm = D_SSM // OUT_TK

    @pl.when(k == 0)
    def _():
        acc_ref[...] = jnp.dot(ys_ref[...], w_ref[...], preferred_element_type=F32)

    @pl.when((k > 0) & (k < n_ssm))
    def _():
        acc_ref[...] += jnp.dot(ys_ref[...], w_ref[...], preferred_element_type=F32)

    @pl.when(k >= n_ssm)
    def _():
        acc_ref[...] += jnp.dot(ya_ref[...], w_ref[...], preferred_element_type=F32)

    @pl.when(k == pl.num_programs(1) - 1)
    def _():
        chunk = 128

        def body(c, carry):
            sl = pl.ds(pl.multiple_of(c * chunk, chunk), chunk)
            h = x_ref[sl, :] + _rms_scale(acc_ref[sl, :], gpost_ref[...])
            h_ref[sl, :] = h
            u_ref[sl, :] = _rms_scale(h, gpre_ref[...]).astype(BF16)
            return carry
        lax.fori_loop(0, OUT_TM // chunk, body, 0)


def _out_projection(y_ssm, y_att, w_out, x2, g_post, g_pre):
    tokens = x2.shape[0]
    n_ssm = D_SSM // OUT_TK
    return pl.pallas_call(
        _outproj_kernel,
        grid=(tokens // OUT_TM, D_MIX // OUT_TK),
        in_specs=[
            pl.BlockSpec((OUT_TM, OUT_TK), lambda i, k: (i, jnp.minimum(k, n_ssm - 1))),
            pl.BlockSpec((OUT_TM, OUT_TK), lambda i, k: (i, jnp.maximum(k - n_ssm, 0))),
            pl.BlockSpec((OUT_TK, D_MODEL), lambda i, k: (k, 0)),
            pl.BlockSpec((OUT_TM, D_MODEL), lambda i, k: (i, 0)),
            pl.BlockSpec((1, D_MODEL), lambda i, k: (0, 0)),
            pl.BlockSpec((1, D_MODEL), lambda i, k: (0, 0)),
        ],
        out_specs=[
            pl.BlockSpec((OUT_TM, D_MODEL), lambda i, k: (i, 0)),
            pl.BlockSpec((OUT_TM, D_MODEL), lambda i, k: (i, 0)),
        ],
        out_shape=[
            jax.ShapeDtypeStruct((tokens, D_MODEL), F32),
            jax.ShapeDtypeStruct((tokens, D_MODEL), BF16),
        ],
        scratch_shapes=[pltpu.VMEM((OUT_TM, D_MODEL), F32)],
        compiler_params=_params(("arbitrary", "arbitrary")),
        name="outproj",
    )(y_ssm, y_att, w_out, x2, g_post, g_pre)


MLP_TM = 512
MLP_TF = 1024


def _mlp_kernel(u_ref, wup_ref, wdown_ref, h_ref, g_ref, o_ref, acc_ref):
    f = pl.program_id(1)
    hid = jnp.maximum(jnp.dot(u_ref[...], wup_ref[...], preferred_element_type=F32), 0.0)
    part = jnp.dot((hid * hid).astype(BF16), wdown_ref[...], preferred_element_type=F32)

    @pl.when(f == 0)
    def _():
        acc_ref[...] = part

    @pl.when(f > 0)
    def _():
        acc_ref[...] += part

    @pl.when(f == pl.num_programs(1) - 1)
    def _():
        chunk = 128

        def body(c, carry):
            sl = pl.ds(pl.multiple_of(c * chunk, chunk), chunk)
            o_ref[sl, :] = h_ref[sl, :] + _rms_scale(acc_ref[sl, :], g_ref[...])
            return carry
        lax.fori_loop(0, MLP_TM // chunk, body, 0)


def _mlp(u2, w_up, w_down, h1, g_post):
    tokens = u2.shape[0]
    return pl.pallas_call(
        _mlp_kernel,
        grid=(tokens // MLP_TM, D_FF // MLP_TF),
        in_specs=[
            pl.BlockSpec((MLP_TM, D_MODEL), lambda i, f: (i, 0)),
            pl.BlockSpec((D_MODEL, MLP_TF), lambda i, f: (0, f)),
            pl.BlockSpec((MLP_TF, D_MODEL), lambda i, f: (f, 0)),
            pl.BlockSpec((MLP_TM, D_MODEL), lambda i, f: (i, 0)),
            pl.BlockSpec((1, D_MODEL), lambda i, f: (0, 0)),
        ],
        out_specs=pl.BlockSpec((MLP_TM, D_MODEL), lambda i, f: (i, 0)),
        out_shape=jax.ShapeDtypeStruct((tokens, D_MODEL), F32),
        scratch_shapes=[pltpu.VMEM((MLP_TM, D_MODEL), F32)],
        compiler_params=_params(("arbitrary", "arbitrary")),
        name="mlp",
    )(u2, w_up, w_down, h1, g_post)


def kernel(x, norm_mix_pre, w_in, conv_w, conv_b, dt_bias, a_log, d_skip, ssm_norm_w, w_out,
           norm_mix_post, norm_mlp_pre, w_up, w_down, norm_mlp_post):
    batch, seq, _ = x.shape
    depth = w_in.shape[0]
    ssm_cols = D_SSM + D_SSM + D_BC
    h = x.reshape(batch * seq, D_MODEL)
    for layer in range(depth):
        w = w_in[layer]
        w_main = jnp.concatenate([w[:, :ssm_cols], w[:, ssm_cols + SSM_HEADS:]], axis=1).astype(BF16)
        w_dt = jnp.pad(w[:, ssm_cols:ssm_cols + SSM_HEADS], ((0, 0), (0, LANES - SSM_HEADS))).astype(BF16)
        zxbc, dt_raw, qkv = _in_projection(
            h, norm_mix_pre[layer].reshape(1, D_MODEL), w_main, w_dt, batch, seq)
        y_ssm = _ssd(zxbc, dt_raw, conv_w[layer], conv_b[layer], dt_bias[layer], a_log[layer],
                     d_skip[layer], ssm_norm_w[layer], batch, seq)
        y_att = _attention(qkv, batch, seq).reshape(batch * seq, D_ATT)
        h1, u2 = _out_projection(
            y_ssm, y_att, w_out[layer].astype(BF16), h,
            norm_mix_post[layer].reshape(1, D_MODEL), norm_mlp_pre[layer].reshape(1, D_MODEL))
        h = _mlp(u2, w_up[layer].astype(BF16), w_down[layer].astype(BF16), h1,
                 norm_mlp_post[layer].reshape(1, D_MODEL))
    return h.reshape(batch, seq, D_MODEL)
```

```python
import functools

import numpy as np
import jax
import jax.numpy as jnp
from jax import lax
from jax.experimental import pallas as pl
from jax.experimental.pallas import tpu as pltpu

F32 = jnp.float32
BF16 = jnp.bfloat16

D_MODEL = 2048
SSM_HEAD_DIM = 64
SSM_HEADS = 32
SSM_GROUPS = 8
HEADS_PER_GROUP = 4
D_STATE = 128
D_SSM = SSM_HEADS * SSM_HEAD_DIM
D_BC = 2 * SSM_GROUPS * D_STATE
GROUP_WIDTH = HEADS_PER_GROUP * SSM_HEAD_DIM
CONV_WIDTH = 4
CHUNK = 128
ATT_HEADS = 16
ATT_HEAD_DIM = 128
D_ATT = ATT_HEADS * ATT_HEAD_DIM
D_MIX = D_SSM + D_ATT
D_FF = 4 * D_MODEL
WINDOW_REACH = 128
MAX_DILATION = 16
EPS = 1e-6
ATT_SCALE = ATT_HEAD_DIM ** -0.5
MASKED = -1e30

LANES = 128
VMEM_LIMIT = 56 * 1024 * 1024


def _params(semantics):
    return pltpu.CompilerParams(dimension_semantics=semantics, vmem_limit_bytes=VMEM_LIMIT)


def _silu(v):
    return v * (1.0 / (1.0 + jnp.exp(-v)))


def _split3(v):
    hi = v.astype(BF16)
    rest = v - hi.astype(F32)
    mid = rest.astype(BF16)
    lo = (rest - mid.astype(F32)).astype(BF16)
    return hi, mid, lo


def _rms_scale(v, gain):
    ms = jnp.mean(v * v, axis=-1, keepdims=True)
    return (v * lax.rsqrt(ms + EPS)) * gain


IN_TM = 1024
IN_TN = 1024
IN_STAGE_SLABS = 4


def _inproj_kernel(x_ref, g_ref, w_ref, wdt_ref, zxbc_ref, dt_ref, qkv_ref, u_ref, uperm_ref, uf_ref,
                   *, n_ssm_tiles):
    j = pl.program_id(1)
    chunk = 128
    n_slabs = D_MODEL // LANES
    rows = IN_TM // MAX_DILATION

    @pl.when(j == 0)
    def _():
        def body(c, carry):
            sl = pl.ds(pl.multiple_of(c * chunk, chunk), chunk)
            u_ref[sl, :] = _rms_scale(x_ref[sl, :], g_ref[...]).astype(BF16)
            return carry
        lax.fori_loop(0, IN_TM // chunk, body, 0)
        dt_ref[...] = jnp.dot(u_ref[...], wdt_ref[...], preferred_element_type=F32)
        for s0 in range(0, n_slabs, IN_STAGE_SLABS):
            for s in range(IN_STAGE_SLABS):
                uf_ref[s] = u_ref[:, (s0 + s) * LANES:(s0 + s + 1) * LANES].astype(F32)
            for r in range(MAX_DILATION):
                for s in range(IN_STAGE_SLABS):
                    uperm_ref[r * rows:(r + 1) * rows, (s0 + s) * LANES:(s0 + s + 1) * LANES] = (
                        uf_ref[s, pl.ds(r, rows, stride=MAX_DILATION), :].astype(BF16))

    @pl.when(j < n_ssm_tiles)
    def _():
        zxbc_ref[...] = jnp.dot(u_ref[...], w_ref[...], preferred_element_type=F32)

    @pl.when(j >= n_ssm_tiles)
    def _():
        scale = jnp.where(j < n_ssm_tiles + D_ATT // IN_TN, ATT_SCALE, 1.0).astype(F32)
        res = jnp.dot(uperm_ref[...], w_ref[...], preferred_element_type=F32) * scale
        for r in range(MAX_DILATION):
            qkv_ref[r] = res[r * rows:(r + 1) * rows, :].astype(BF16)


def _in_projection(x2, gain, w_main, w_dt, batch, seq):
    tokens = x2.shape[0]
    n_m = tokens // IN_TM
    m_per_seq = seq // IN_TM
    ssm_cols = 2 * D_SSM + D_BC
    att_cols = 3 * D_ATT
    n_ssm_tiles = ssm_cols // IN_TN
    n_att_tiles = att_cols // IN_TN
    run_len = seq // MAX_DILATION
    return pl.pallas_call(
        functools.partial(_inproj_kernel, n_ssm_tiles=n_ssm_tiles),
        grid=(n_m, n_ssm_tiles + n_att_tiles),
        in_specs=[
            pl.BlockSpec((IN_TM, D_MODEL), lambda i, j: (i, 0)),
            pl.BlockSpec((1, D_MODEL), lambda i, j: (0, 0)),
            pl.BlockSpec((D_MODEL, IN_TN), lambda i, j: (0, j)),
            pl.BlockSpec((D_MODEL, LANES), lambda i, j: (0, 0)),
        ],
        out_specs=[
            pl.BlockSpec((IN_TM, IN_TN), lambda i, j: (i, jnp.minimum(j, n_ssm_tiles - 1))),
            pl.BlockSpec((IN_TM, LANES), lambda i, j: (i, 0)),
            pl.BlockSpec((None, MAX_DILATION, IN_TM // MAX_DILATION, IN_TN),
                         lambda i, j: (i // m_per_seq, 0, i % m_per_seq, jnp.maximum(j - n_ssm_tiles, 0))),
        ],
        out_shape=[
            jax.ShapeDtypeStruct((tokens, ssm_cols), F32),
            jax.ShapeDtypeStruct((tokens, LANES), F32),
            jax.ShapeDtypeStruct((batch, MAX_DILATION, run_len, att_cols), BF16),
        ],
        scratch_shapes=[
            pltpu.VMEM((IN_TM, D_MODEL), BF16),
            pltpu.VMEM((IN_TM, D_MODEL), BF16),
            pltpu.VMEM((IN_STAGE_SLABS, IN_TM, LANES), F32),
        ],
        compiler_params=_params(("arbitrary", "arbitrary")),
        name="inproj",
    )(x2, gain, w_main, w_dt)


CONV_PAD = 8


def _ssd_kernel(z_ref, x_ref, bc_ref, dt_ref, cwx_ref, cbx_ref, cwbc_ref, cbbc_ref,
                dtb_ref, alog_ref, dsk_ref, nw_ref, tri_ref, expand_ref,
                y_ref, state_ref, xpad_ref, bcpad_ref):
    @pl.when(pl.program_id(1) == 0)
    def _():
        state_ref[...] = jnp.zeros_like(state_ref)
        xpad_ref[0:CONV_PAD, :] = jnp.zeros((CONV_PAD, D_SSM), F32)
        bcpad_ref[0:CONV_PAD, :] = jnp.zeros((CONV_PAD, D_BC), F32)

    xpad_ref[CONV_PAD:CONV_PAD + CHUNK, :] = x_ref[...]
    bcpad_ref[CONV_PAD:CONV_PAD + CHUNK, :] = bc_ref[...]

    def conv_silu(pad_ref, w_ref, b_ref, c0, width):
        cols = slice(c0, c0 + width)
        acc = b_ref[:, cols]
        for k in range(CONV_WIDTH):
            start = CONV_PAD - (CONV_WIDTH - 1) + k
            acc = acc + w_ref[k:k + 1, cols] * pad_ref[start:start + CHUNK, cols]
        return _silu(acc)

    dt_in = dt_ref[...] + dtb_ref[...]
    dtv = jnp.maximum(dt_in, 0.0) + jnp.log1p(jnp.exp(-jnp.abs(dt_in)))
    da = dtv * (-jnp.exp(alog_ref[...]))
    tri = tri_ref[...]
    a_cs = sum(jnp.dot(tri, part, preferred_element_type=F32)
               for part in _split3(da))
    a_last = a_cs[CHUNK - 1:CHUNK, :]
    w_end = dtv * jnp.exp(a_last - a_cs)
    a_cs_t = a_cs.T
    dt_t = dtv.T
    w_end_t = w_end.T
    expand = expand_ref[...]
    chunk_decay = sum(jnp.dot(part, expand, preferred_element_type=F32)
                      for part in _split3(jnp.broadcast_to(jnp.exp(a_last), (8, LANES)))
                      )[0:1, :]

    row_i = lax.broadcasted_iota(jnp.int32, (CHUNK, CHUNK), 0)
    col_j = lax.broadcasted_iota(jnp.int32, (CHUNK, CHUNK), 1)
    causal = row_i >= col_j
    lane_head = lax.broadcasted_iota(jnp.int32, (CHUNK, GROUP_WIDTH), 1) // SSM_HEAD_DIM
    own_lanes = [jnp.where(lane_head == r, 1.0, 0.0).astype(BF16) for r in range(HEADS_PER_GROUP)]

    for g in range(SSM_GROUPS):
        gcols = slice(g * GROUP_WIDTH, (g + 1) * GROUP_WIDTH)
        xs = conv_silu(xpad_ref, cwx_ref, cbx_ref, g * GROUP_WIDTH, GROUP_WIDTH)
        bm = conv_silu(bcpad_ref, cwbc_ref, cbbc_ref, g * D_STATE, D_STATE)
        cm = conv_silu(bcpad_ref, cwbc_ref, cbbc_ref, D_BC // 2 + g * D_STATE, D_STATE)
        cb = lax.dot_general(cm.astype(BF16), bm.astype(BF16), (((1,), (1,)), ((), ())),
                             preferred_element_type=F32)
        bm_t = bm.T
        prev = state_ref[g]
        xs16 = xs.astype(BF16)
        prev16 = prev.astype(BF16)

        lhs_y, rhs_y, lhs_s, rhs_s = [], [], [], []
        for r in range(HEADS_PER_GROUP):
            h = g * HEADS_PER_GROUP + r
            col_a = jnp.broadcast_to(a_cs[:, h:h + 1], (CHUNK, CHUNK))
            row_a = a_cs_t[h:h + 1, :]
            decay = jnp.exp(jnp.where(causal, col_a - row_a, -jnp.inf))
            lhs_y.append((cb * decay * dt_t[h:h + 1, :]).astype(BF16))
            lhs_y.append((cm * jnp.exp(col_a)).astype(BF16))
            x_own = xs16 * own_lanes[r]
            rhs_y.append(x_own)
            rhs_y.append(prev16 * own_lanes[r])
            lhs_s.append((bm_t * w_end_t[h:h + 1, :]).astype(BF16))
            rhs_s.append(x_own)

        y = jnp.dot(jnp.concatenate(lhs_y, axis=1), jnp.concatenate(rhs_y, axis=0),
                    preferred_element_type=F32)
        s_new = jnp.dot(jnp.concatenate(lhs_s, axis=1), jnp.concatenate(rhs_s, axis=0),
                        preferred_element_type=F32)
        state_ref[g] = prev * chunk_decay[:, gcols] + s_new

        y = y + dsk_ref[:, gcols] * xs
        gated = y * _silu(z_ref[:, gcols])
        y_ref[:, gcols] = _rms_scale(gated, nw_ref[:, gcols]).astype(BF16)

    xpad_ref[0:CONV_PAD, :] = x_ref[CHUNK - CONV_PAD:CHUNK, :]
    bcpad_ref[0:CONV_PAD, :] = bc_ref[CHUNK - CONV_PAD:CHUNK, :]


def _ssd(zxbc, dt_raw, conv_w, conv_b, dt_bias, a_log, d_skip, ssm_norm_w, batch, seq):
    tokens = zxbc.shape[0]
    n_chunks = seq // CHUNK
    pad_heads = LANES - SSM_HEADS
    dtb = jnp.pad(dt_bias.astype(F32), (0, pad_heads)).reshape(1, LANES)
    alog = jnp.pad(a_log.astype(F32), (0, pad_heads)).reshape(1, LANES)
    dsk = jnp.repeat(d_skip.astype(F32), SSM_HEAD_DIM).reshape(1, D_SSM)
    tri = jnp.asarray(np.tril(np.ones((CHUNK, CHUNK), np.float32)), dtype=BF16)
    expand = np.zeros((LANES, D_SSM), np.float32)
    for h in range(SSM_HEADS):
        expand[h, h * SSM_HEAD_DIM:(h + 1) * SSM_HEAD_DIM] = 1.0
    expand = jnp.asarray(expand, dtype=BF16)

    def rows(b, c):
        return b * n_chunks + c

    def full(shape):
        return pl.BlockSpec(shape, lambda b, c: (0,) * len(shape))

    return pl.pallas_call(
        _ssd_kernel,
        grid=(batch, n_chunks),
        in_specs=[
            pl.BlockSpec((CHUNK, D_SSM), lambda b, c: (rows(b, c), 0)),
            pl.BlockSpec((CHUNK, D_SSM), lambda b, c: (rows(b, c), 1)),
            pl.BlockSpec((CHUNK, D_BC), lambda b, c: (rows(b, c), 2)),
            pl.BlockSpec((CHUNK, LANES), lambda b, c: (rows(b, c), 0)),
            full((CONV_WIDTH, D_SSM)), full((1, D_SSM)),
            full((CONV_WIDTH, D_BC)), full((1, D_BC)),
            full((1, LANES)), full((1, LANES)), full((1, D_SSM)), full((1, D_SSM)),
            full((CHUNK, CHUNK)), full((LANES, D_SSM)),
        ],
        out_specs=pl.BlockSpec((CHUNK, D_SSM), lambda b, c: (rows(b, c), 0)),
        out_shape=jax.ShapeDtypeStruct((tokens, D_SSM), BF16),
        scratch_shapes=[
            pltpu.VMEM((SSM_GROUPS, D_STATE, GROUP_WIDTH), F32),
            pltpu.VMEM((CONV_PAD + CHUNK, D_SSM), F32),
            pltpu.VMEM((CONV_PAD + CHUNK, D_BC), F32),
        ],
        compiler_params=_params(("arbitrary", "arbitrary")),
        name="ssd",
    )(zxbc, zxbc, zxbc, dt_raw,
      conv_w[:, :D_SSM], conv_b[:D_SSM].reshape(1, D_SSM),
      conv_w[:, D_SSM:], conv_b[D_SSM:].reshape(1, D_BC),
      dtb, alog, dsk, ssm_norm_w.reshape(1, D_SSM), tri, expand)


BRANCHES = ((16, 16, 16), (4, 32, 32), (1, 128, 128))
ATT_GROUP = (4, 8, 8)


def _branch_bias(n_runs, width, back):
    s_q = np.repeat(np.arange(n_runs), width)
    i_q = np.tile(np.arange(width), n_runs)
    j_q = n_runs * i_q + s_q
    s_p = np.repeat(np.arange(n_runs), back)
    i_p = np.tile(np.arange(back), n_runs) - back
    j_k = np.concatenate([n_runs * i_p + s_p, j_q])
    dist = j_q[:, None] - j_k[None, :]
    return np.where((dist >= 0) & (dist <= WINDOW_REACH), 0.0, MASKED).astype(np.float32)


def _attn_kernel(q_ref, k_ref, v_ref, b0_ref, b1_ref, b2_ref, o_ref,
                 num_ref, m_ref, l_ref, nat_ref, s0_ref, p0_ref, s1_ref, p1_ref, *, run_len):
    bias_refs = (b0_ref, b1_ref, b2_ref)
    stage_refs = ((s0_ref, p0_ref), (s1_ref, p1_ref), (s1_ref, p1_ref))

    def rows_of(ref, starts, size):
        parts = [ref[st:st + size, :] for st in starts]
        return parts[0] if len(parts) == 1 else jnp.concatenate(parts, axis=0)

    def keys_of(ref, branch, starts, with_prev):
        n_runs, width, back = BRANCHES[branch]
        cur = rows_of(ref, starts, width)
        if not with_prev:
            return cur
        return jnp.concatenate([rows_of(ref, [st - back for st in starts], back), cur], axis=0)

    def layout(branch, blocks, slot):
        n_runs, width, back = BRANCHES[branch]
        n_cls = MAX_DILATION // n_runs
        out = []
        for u, (cls, i0, with_prev) in enumerate(blocks):
            starts = [(cls + n_cls * s) * run_len + i0 for s in range(n_runs)]
            out.append((slot * ATT_GROUP[branch] + u, starts, with_prev,
                        n_runs * (width + (back if with_prev else 0))))
        return out

    def issue(branch, blocks, slot):
        n_runs, width, back = BRANCHES[branch]
        s_ref, _ = stage_refs[branch]
        for idx, starts, with_prev, n_keys in layout(branch, blocks, slot):
            q = rows_of(q_ref, starts, width)
            k = keys_of(k_ref, branch, starts, with_prev)
            bias = bias_refs[branch][...] if with_prev else bias_refs[branch][:, n_runs * back:]
            s_ref[idx, :, 0:n_keys] = lax.dot_general(
                q, k, (((1,), (1,)), ((), ())), preferred_element_type=F32) + bias

    def finish(branch, blocks, slot):
        n_runs, width, back = BRANCHES[branch]
        s_ref, p_ref = stage_refs[branch]
        plan = layout(branch, blocks, slot)
        for idx, starts, with_prev, n_keys in plan:
            s = s_ref[idx, :, 0:n_keys]
            m = jnp.max(s, axis=-1, keepdims=True)
            p_ref[idx, :, 0:n_keys] = jnp.exp(s - m).astype(BF16)
            for part, st in enumerate(starts):
                m_ref[branch, st:st + width, :] = jnp.broadcast_to(
                    m[part * width:(part + 1) * width], (width, ATT_HEAD_DIM))
        for idx, starts, with_prev, n_keys in plan:
            v = keys_of(v_ref, branch, starts, with_prev)
            v1 = jnp.concatenate([v, jnp.ones((n_keys, ATT_HEAD_DIM), BF16)], axis=1)
            pv = jnp.dot(p_ref[idx, :, 0:n_keys], v1, preferred_element_type=F32)
            for part, st in enumerate(starts):
                rows = slice(part * width, (part + 1) * width)
                num_ref[branch, st:st + width, :] = pv[rows, 0:ATT_HEAD_DIM]
                l_ref[branch, st:st + width, :] = pv[rows, ATT_HEAD_DIM:]

    groups = []
    for branch, (n_runs, width, back) in enumerate(BRANCHES):
        blocks = [(cls, i0, i0 > 0) for cls in range(MAX_DILATION // n_runs) for i0 in range(0, run_len, width)]
        size = ATT_GROUP[branch]
        groups += [(branch, blocks[g:g + size]) for g in range(0, len(blocks), size)]
    issue(*groups[0], 0)
    for n, group in enumerate(groups):
        if n + 1 < len(groups):
            issue(*groups[n + 1], (n + 1) % 2)
        finish(*group, n % 2)

    def combine(r, carry):
        sl = pl.ds(pl.multiple_of(r * run_len, run_len), run_len)
        m0, m1, m2 = m_ref[0, sl, :], m_ref[1, sl, :], m_ref[2, sl, :]
        top = jnp.maximum(jnp.maximum(m0, m1), m2)
        w0, w1, w2 = jnp.exp(m0 - top), jnp.exp(m1 - top), jnp.exp(m2 - top)
        den = w0 * l_ref[0, sl, :] + w1 * l_ref[1, sl, :] + w2 * l_ref[2, sl, :]
        num = w0 * num_ref[0, sl, :] + w1 * num_ref[1, sl, :] + w2 * num_ref[2, sl, :]
        nat_ref[pl.ds(r, run_len, stride=MAX_DILATION), :] = num / den
        return carry
    lax.fori_loop(0, MAX_DILATION, combine, 0)

    rows = 256

    def emit(c, carry):
        sl = pl.ds(pl.multiple_of(c * rows, rows), rows)
        o_ref[sl, :] = nat_ref[sl, :].astype(BF16)
        return carry
    lax.fori_loop(0, (MAX_DILATION * run_len) // rows, emit, 0)


def _stage_scratch():
    out = []
    for branch in (0, 1):
        n_runs, width, back = BRANCHES[branch]
        shape = (2 * ATT_GROUP[branch], n_runs * width, n_runs * (width + back))
        out += [pltpu.VMEM(shape, F32), pltpu.VMEM(shape, BF16)]
    return out


def _attention(qkv, batch, seq):
    run_len = seq // MAX_DILATION
    qkv2 = qkv.reshape(batch, seq, 3 * D_ATT)
    biases = [jnp.asarray(_branch_bias(*b)) for b in BRANCHES]

    def head_block(offset):
        return pl.BlockSpec((None, seq, ATT_HEAD_DIM), lambda b, h: (b, 0, offset + h))

    def full(shape):
        return pl.BlockSpec(shape, lambda b, h: (0,) * len(shape))

    return pl.pallas_call(
        functools.partial(_attn_kernel, run_len=run_len),
        grid=(batch, ATT_HEADS),
        in_specs=[head_block(0), head_block(ATT_HEADS), head_block(2 * ATT_HEADS)]
        + [full(b.shape) for b in biases],
        out_specs=pl.BlockSpec((None, seq, ATT_HEAD_DIM), lambda b, h: (b, 0, h)),
        out_shape=jax.ShapeDtypeStruct((batch, seq, D_ATT), BF16),
        scratch_shapes=[
            pltpu.VMEM((len(BRANCHES), seq, ATT_HEAD_DIM), F32),
            pltpu.VMEM((len(BRANCHES), seq, ATT_HEAD_DIM), F32),
            pltpu.VMEM((len(BRANCHES), seq, ATT_HEAD_DIM), F32),
            pltpu.VMEM((seq, ATT_HEAD_DIM), F32),
        ] + _stage_scratch(),
        compiler_params=_params(("arbitrary", "arbitrary")),
        name="dilated_attn",
    )(qkv2, qkv2, qkv2, *biases)


OUT_TM = 512


def _outproj_kernel(ys_ref, ya_ref, w_ref, x_ref, gpost_ref, gpre_ref, h_ref, u_ref, mix_ref):
    mix_ref[...] = jnp.dot(ys_ref[...], w_ref[0:D_SSM, :], preferred_element_type=F32)
    mix_ref[...] += jnp.dot(ya_ref[...], w_ref[D_SSM:D_MIX, :], preferred_element_type=F32)
    chunk = 128

    def body(c, carry):
        sl = pl.ds(pl.multiple_of(c * chunk, chunk), chunk)
        h = x_ref[sl, :] + _rms_scale(mix_ref[sl, :], gpost_ref[...])
        h_ref[sl, :] = h
        u_ref[sl, :] = _rms_scale(h, gpre_ref[...]).astype(BF16)
        return carry
    lax.fori_loop(0, OUT_TM // chunk, body, 0)


def _out_projection(y_ssm, y_att, w_out, x2, g_post, g_pre):
    tokens = x2.shape[0]

    def rows(width):
        return pl.BlockSpec((OUT_TM, width), lambda i: (i, 0))

    def whole(shape):
        return pl.BlockSpec(shape, lambda i: (0, 0), pipeline_mode=pl.Buffered(1))

    return pl.pallas_call(
        _outproj_kernel,
        grid=(tokens // OUT_TM,),
        in_specs=[rows(D_SSM), rows(D_ATT), whole((D_MIX, D_MODEL)), rows(D_MODEL),
                  whole((1, D_MODEL)), whole((1, D_MODEL))],
        out_specs=[rows(D_MODEL), rows(D_MODEL)],
        out_shape=[
            jax.ShapeDtypeStruct((tokens, D_MODEL), F32),
            jax.ShapeDtypeStruct((tokens, D_MODEL), BF16),
        ],
        scratch_shapes=[pltpu.VMEM((OUT_TM, D_MODEL), F32)],
        compiler_params=_params(("arbitrary",)),
        name="outproj",
    )(y_ssm, y_att, w_out, x2, g_post, g_pre)


MLP_TM = 512
MLP_TF = 1024


def _mlp_kernel(u_ref, wup_ref, wdown_ref, h_ref, g_ref, o_ref, acc_ref):
    f = pl.program_id(1)
    hid = jnp.maximum(jnp.dot(u_ref[...], wup_ref[...], preferred_element_type=F32), 0.0)
    part = jnp.dot((hid * hid).astype(BF16), wdown_ref[...], preferred_element_type=F32)

    @pl.when(f == 0)
    def _():
        acc_ref[...] = part

    @pl.when(f > 0)
    def _():
        acc_ref[...] += part

    @pl.when(f == pl.num_programs(1) - 1)
    def _():
        chunk = 128

        def body(c, carry):
            sl = pl.ds(pl.multiple_of(c * chunk, chunk), chunk)
            o_ref[sl, :] = h_ref[sl, :] + _rms_scale(acc_ref[sl, :], g_ref[...])
            return carry
        lax.fori_loop(0, MLP_TM // chunk, body, 0)


def _mlp(u2, w_up, w_down, h1, g_post):
    tokens = u2.shape[0]
    return pl.pallas_call(
        _mlp_kernel,
        grid=(tokens // MLP_TM, D_FF // MLP_TF),
        in_specs=[
            pl.BlockSpec((MLP_TM, D_MODEL), lambda i, f: (i, 0)),
            pl.BlockSpec((D_MODEL, MLP_TF), lambda i, f: (0, f)),
            pl.BlockSpec((MLP_TF, D_MODEL), lambda i, f: (f, 0)),
            pl.BlockSpec((MLP_TM, D_MODEL), lambda i, f: (i, 0)),
            pl.BlockSpec((1, D_MODEL), lambda i, f: (0, 0)),
        ],
        out_specs=pl.BlockSpec((MLP_TM, D_MODEL), lambda i, f: (i, 0)),
        out_shape=jax.ShapeDtypeStruct((tokens, D_MODEL), F32),
        scratch_shapes=[pltpu.VMEM((MLP_TM, D_MODEL), F32)],
        compiler_params=_params(("arbitrary", "arbitrary")),
        name="mlp",
    )(u2, w_up, w_down, h1, g_post)


def kernel(x, norm_mix_pre, w_in, conv_w, conv_b, dt_bias, a_log, d_skip, ssm_norm_w, w_out,
           norm_mix_post, norm_mlp_pre, w_up, w_down, norm_mlp_post):
    batch, seq, _ = x.shape
    depth = w_in.shape[0]
    ssm_cols = D_SSM + D_SSM + D_BC
    h = x.reshape(batch * seq, D_MODEL)
    for layer in range(depth):
        w = w_in[layer]
        w_main = jnp.concatenate([w[:, :ssm_cols], w[:, ssm_cols + SSM_HEADS:]], axis=1).astype(BF16)
        w_dt = jnp.pad(w[:, ssm_cols:ssm_cols + SSM_HEADS], ((0, 0), (0, LANES - SSM_HEADS))).astype(BF16)
        zxbc, dt_raw, qkv = _in_projection(
            h, norm_mix_pre[layer].reshape(1, D_MODEL), w_main, w_dt, batch, seq)
        y_ssm = _ssd(zxbc, dt_raw, conv_w[layer], conv_b[layer], dt_bias[layer], a_log[layer],
                     d_skip[layer], ssm_norm_w[layer], batch, seq)
        y_att = _attention(qkv, batch, seq).reshape(batch * seq, D_ATT)
        h1, u2 = _out_projection(
            y_ssm, y_att, w_out[layer].astype(BF16), h,
            norm_mix_post[layer].reshape(1, D_MODEL), norm_mlp_pre[layer].reshape(1, D_MODEL))
        h = _mlp(u2, w_up[layer].astype(BF16), w_down[layer].astype(BF16), h1,
                 norm_mlp_post[layer].reshape(1, D_MODEL))
    return h.reshape(batch, seq, D_MODEL)
```

```python
import functools

import numpy as np
import jax
import jax.numpy as jnp
from jax import lax
from jax.experimental import pallas as pl
from jax.experimental.pallas import tpu as pltpu

F32 = jnp.float32
BF16 = jnp.bfloat16

D_MODEL = 2048
SSM_HEAD_DIM = 64
SSM_HEADS = 32
SSM_GROUPS = 8
HEADS_PER_GROUP = 4
D_STATE = 128
D_SSM = SSM_HEADS * SSM_HEAD_DIM
D_BC = 2 * SSM_GROUPS * D_STATE
GROUP_WIDTH = HEADS_PER_GROUP * SSM_HEAD_DIM
CONV_WIDTH = 4
CHUNK = 128
ATT_HEADS = 16
ATT_HEAD_DIM = 128
D_ATT = ATT_HEADS * ATT_HEAD_DIM
D_MIX = D_SSM + D_ATT
D_FF = 4 * D_MODEL
WINDOW_REACH = 128
MAX_DILATION = 16
EPS = 1e-6
ATT_SCALE = ATT_HEAD_DIM ** -0.5
MASKED = -1e30

LANES = 128
VMEM_LIMIT = 56 * 1024 * 1024


def _params(semantics):
    return pltpu.CompilerParams(dimension_semantics=semantics, vmem_limit_bytes=VMEM_LIMIT)


def _silu(v):
    return v * (1.0 / (1.0 + jnp.exp(-v)))


def _split3(v):
    hi = v.astype(BF16)
    rest = v - hi.astype(F32)
    mid = rest.astype(BF16)
    lo = (rest - mid.astype(F32)).astype(BF16)
    return hi, mid, lo


def _rms_scale(v, gain):
    ms = jnp.mean(v * v, axis=-1, keepdims=True)
    return (v * lax.rsqrt(ms + EPS)) * gain


IN_TM = 1024
IN_TN = 1024
IN_STAGE_SLABS = 4


def _inproj_kernel(x_ref, g_ref, w_ref, wdt_ref, zxbc_ref, dt_ref, qkv_ref, u_ref, uperm_ref, uf_ref,
                   *, n_ssm_tiles):
    j = pl.program_id(1)
    chunk = 128
    n_slabs = D_MODEL // LANES
    rows = IN_TM // MAX_DILATION

    @pl.when(j == 0)
    def _():
        def body(c, carry):
            sl = pl.ds(pl.multiple_of(c * chunk, chunk), chunk)
            u_ref[sl, :] = _rms_scale(x_ref[sl, :], g_ref[...]).astype(BF16)
            return carry
        lax.fori_loop(0, IN_TM // chunk, body, 0)
        dt_ref[...] = jnp.dot(u_ref[...], wdt_ref[...], preferred_element_type=F32)
        for s0 in range(0, n_slabs, IN_STAGE_SLABS):
            for s in range(IN_STAGE_SLABS):
                uf_ref[s] = u_ref[:, (s0 + s) * LANES:(s0 + s + 1) * LANES].astype(F32)
            for r in range(MAX_DILATION):
                for s in range(IN_STAGE_SLABS):
                    uperm_ref[r * rows:(r + 1) * rows, (s0 + s) * LANES:(s0 + s + 1) * LANES] = (
                        uf_ref[s, pl.ds(r, rows, stride=MAX_DILATION), :].astype(BF16))

    @pl.when(j < n_ssm_tiles)
    def _():
        zxbc_ref[...] = jnp.dot(u_ref[...], w_ref[...], preferred_element_type=F32)

    @pl.when(j >= n_ssm_tiles)
    def _():
        scale = jnp.where(j < n_ssm_tiles + D_ATT // IN_TN, ATT_SCALE, 1.0).astype(F32)
        res = jnp.dot(uperm_ref[...], w_ref[...], preferred_element_type=F32) * scale
        for r in range(MAX_DILATION):
            qkv_ref[r] = res[r * rows:(r + 1) * rows, :].astype(BF16)


def _in_projection(x2, gain, w_main, w_dt, batch, seq):
    tokens = x2.shape[0]
    n_m = tokens // IN_TM
    m_per_seq = seq // IN_TM
    ssm_cols = 2 * D_SSM + D_BC
    att_cols = 3 * D_ATT
    n_ssm_tiles = ssm_cols // IN_TN
    n_att_tiles = att_cols // IN_TN
    run_len = seq // MAX_DILATION
    return pl.pallas_call(
        functools.partial(_inproj_kernel, n_ssm_tiles=n_ssm_tiles),
        grid=(n_m, n_ssm_tiles + n_att_tiles),
        in_specs=[
            pl.BlockSpec((IN_TM, D_MODEL), lambda i, j: (i, 0)),
            pl.BlockSpec((1, D_MODEL), lambda i, j: (0, 0)),
            pl.BlockSpec((D_MODEL, IN_TN), lambda i, j: (0, j)),
            pl.BlockSpec((D_MODEL, LANES), lambda i, j: (0, 0)),
        ],
        out_specs=[
            pl.BlockSpec((IN_TM, IN_TN), lambda i, j: (i, jnp.minimum(j, n_ssm_tiles - 1))),
            pl.BlockSpec((IN_TM, LANES), lambda i, j: (i, 0)),
            pl.BlockSpec((None, MAX_DILATION, IN_TM // MAX_DILATION, IN_TN),
                         lambda i, j: (i // m_per_seq, 0, i % m_per_seq, jnp.maximum(j - n_ssm_tiles, 0))),
        ],
        out_shape=[
            jax.ShapeDtypeStruct((tokens, ssm_cols), F32),
            jax.ShapeDtypeStruct((tokens, LANES), F32),
            jax.ShapeDtypeStruct((batch, MAX_DILATION, run_len, att_cols), BF16),
        ],
        scratch_shapes=[
            pltpu.VMEM((IN_TM, D_MODEL), BF16),
            pltpu.VMEM((IN_TM, D_MODEL), BF16),
            pltpu.VMEM((IN_STAGE_SLABS, IN_TM, LANES), F32),
        ],
        compiler_params=_params(("arbitrary", "arbitrary")),
        name="inproj",
    )(x2, gain, w_main, w_dt)


CONV_PAD = 8


def _ssd_kernel(z_ref, x_ref, bc_ref, dt_ref, cwx_ref, cbx_ref, cwbc_ref, cbbc_ref,
                dtb_ref, alog_ref, dsk_ref, nw_ref, tri_ref, expand_ref,
                y_ref, state_ref, xpad_ref, bcpad_ref):
    @pl.when(pl.program_id(1) == 0)
    def _():
        state_ref[...] = jnp.zeros_like(state_ref)
        xpad_ref[0:CONV_PAD, :] = jnp.zeros((CONV_PAD, D_SSM), F32)
        bcpad_ref[0:CONV_PAD, :] = jnp.zeros((CONV_PAD, D_BC), F32)

    xpad_ref[CONV_PAD:CONV_PAD + CHUNK, :] = x_ref[...]
    bcpad_ref[CONV_PAD:CONV_PAD + CHUNK, :] = bc_ref[...]

    def conv_silu(pad_ref, w_ref, b_ref, c0, width):
        cols = slice(c0, c0 + width)
        acc = b_ref[:, cols]
        for k in range(CONV_WIDTH):
            start = CONV_PAD - (CONV_WIDTH - 1) + k
            acc = acc + w_ref[k:k + 1, cols] * pad_ref[start:start + CHUNK, cols]
        return _silu(acc)

    dt_in = dt_ref[...] + dtb_ref[...]
    dtv = jnp.maximum(dt_in, 0.0) + jnp.log1p(jnp.exp(-jnp.abs(dt_in)))
    da = dtv * (-jnp.exp(alog_ref[...]))
    tri = tri_ref[...]
    a_cs = sum(jnp.dot(tri, part, preferred_element_type=F32)
               for part in _split3(da))
    a_last = a_cs[CHUNK - 1:CHUNK, :]
    w_end = dtv * jnp.exp(a_last - a_cs)
    a_cs_t = a_cs.T
    dt_t = dtv.T
    w_end_t = w_end.T
    expand = expand_ref[...]
    chunk_decay = sum(jnp.dot(part, expand, preferred_element_type=F32)
                      for part in _split3(jnp.broadcast_to(jnp.exp(a_last), (8, LANES)))
                      )[0:1, :]

    row_i = lax.broadcasted_iota(jnp.int32, (CHUNK, CHUNK), 0)
    col_j = lax.broadcasted_iota(jnp.int32, (CHUNK, CHUNK), 1)
    causal = row_i >= col_j
    lane_head = lax.broadcasted_iota(jnp.int32, (CHUNK, GROUP_WIDTH), 1) // SSM_HEAD_DIM

    def own_lanes(parts):
        out = parts[-1]
        for r in range(HEADS_PER_GROUP - 2, -1, -1):
            out = jnp.where(lane_head == r, parts[r], out)
        return out

    for g in range(SSM_GROUPS):
        gcols = slice(g * GROUP_WIDTH, (g + 1) * GROUP_WIDTH)
        xs = conv_silu(xpad_ref, cwx_ref, cbx_ref, g * GROUP_WIDTH, GROUP_WIDTH)
        bm = conv_silu(bcpad_ref, cwbc_ref, cbbc_ref, g * D_STATE, D_STATE)
        cm = conv_silu(bcpad_ref, cwbc_ref, cbbc_ref, D_BC // 2 + g * D_STATE, D_STATE)
        xs16 = xs.astype(BF16)
        cb = lax.dot_general(cm.astype(BF16), bm.astype(BF16), (((1,), (1,)), ((), ())),
                             preferred_element_type=F32)
        bm_t = bm.T
        prev = state_ref[g]
        rhs_y = jnp.concatenate([xs16, prev.astype(BF16)], axis=0)

        y_parts, s_parts = [], []
        for r in range(HEADS_PER_GROUP):
            h = g * HEADS_PER_GROUP + r
            col_a = jnp.broadcast_to(a_cs[:, h:h + 1], (CHUNK, CHUNK))
            row_a = a_cs_t[h:h + 1, :]
            decay = jnp.exp(jnp.where(causal, col_a - row_a, -jnp.inf))
            lhs = jnp.concatenate([(cb * decay * dt_t[h:h + 1, :]).astype(BF16),
                                   (cm * jnp.exp(col_a)).astype(BF16)], axis=1)
            y_parts.append(jnp.dot(lhs, rhs_y, preferred_element_type=F32))
            s_parts.append(jnp.dot((bm_t * w_end_t[h:h + 1, :]).astype(BF16), xs16,
                                   preferred_element_type=F32))

        state_ref[g] = prev * chunk_decay[:, gcols] + own_lanes(s_parts)
        y = own_lanes(y_parts) + dsk_ref[:, gcols] * xs
        gated = y * _silu(z_ref[:, gcols])
        y_ref[:, gcols] = _rms_scale(gated, nw_ref[:, gcols]).astype(BF16)

    xpad_ref[0:CONV_PAD, :] = x_ref[CHUNK - CONV_PAD:CHUNK, :]
    bcpad_ref[0:CONV_PAD, :] = bc_ref[CHUNK - CONV_PAD:CHUNK, :]


def _ssd(zxbc, dt_raw, conv_w, conv_b, dt_bias, a_log, d_skip, ssm_norm_w, batch, seq):
    tokens = zxbc.shape[0]
    n_chunks = seq // CHUNK
    pad_heads = LANES - SSM_HEADS
    dtb = jnp.pad(dt_bias.astype(F32), (0, pad_heads)).reshape(1, LANES)
    alog = jnp.pad(a_log.astype(F32), (0, pad_heads)).reshape(1, LANES)
    dsk = jnp.repeat(d_skip.astype(F32), SSM_HEAD_DIM).reshape(1, D_SSM)
    tri = jnp.asarray(np.tril(np.ones((CHUNK, CHUNK), np.float32)), dtype=BF16)
    expand = np.zeros((LANES, D_SSM), np.float32)
    for h in range(SSM_HEADS):
        expand[h, h * SSM_HEAD_DIM:(h + 1) * SSM_HEAD_DIM] = 1.0
    expand = jnp.asarray(expand, dtype=BF16)

    def rows(b, c):
        return b * n_chunks + c

    def full(shape):
        return pl.BlockSpec(shape, lambda b, c: (0,) * len(shape))

    return pl.pallas_call(
        _ssd_kernel,
        grid=(batch, n_chunks),
        in_specs=[
            pl.BlockSpec((CHUNK, D_SSM), lambda b, c: (rows(b, c), 0)),
            pl.BlockSpec((CHUNK, D_SSM), lambda b, c: (rows(b, c), 1)),
            pl.BlockSpec((CHUNK, D_BC), lambda b, c: (rows(b, c), 2)),
            pl.BlockSpec((CHUNK, LANES), lambda b, c: (rows(b, c), 0)),
            full((CONV_WIDTH, D_SSM)), full((1, D_SSM)),
            full((CONV_WIDTH, D_BC)), full((1, D_BC)),
            full((1, LANES)), full((1, LANES)), full((1, D_SSM)), full((1, D_SSM)),
            full((CHUNK, CHUNK)), full((LANES, D_SSM)),
        ],
        out_specs=pl.BlockSpec((CHUNK, D_SSM), lambda b, c: (rows(b, c), 0)),
        out_shape=jax.ShapeDtypeStruct((tokens, D_SSM), BF16),
        scratch_shapes=[
            pltpu.VMEM((SSM_GROUPS, D_STATE, GROUP_WIDTH), F32),
            pltpu.VMEM((CONV_PAD + CHUNK, D_SSM), F32),
            pltpu.VMEM((CONV_PAD + CHUNK, D_BC), F32),
        ],
        compiler_params=_params(("arbitrary", "arbitrary")),
        name="ssd",
    )(zxbc, zxbc, zxbc, dt_raw,
      conv_w[:, :D_SSM], conv_b[:D_SSM].reshape(1, D_SSM),
      conv_w[:, D_SSM:], conv_b[D_SSM:].reshape(1, D_BC),
      dtb, alog, dsk, ssm_norm_w.reshape(1, D_SSM), tri, expand)


BRANCHES = ((16, 16, 16), (4, 32, 32), (1, 128, 128))
ATT_GROUP = (4, 8, 8)


def _branch_bias(n_runs, width, back):
    s_q = np.repeat(np.arange(n_runs), width)
    i_q = np.tile(np.arange(width), n_runs)
    j_q = n_runs * i_q + s_q
    s_p = np.repeat(np.arange(n_runs), back)
    i_p = np.tile(np.arange(back), n_runs) - back
    j_k = np.concatenate([n_runs * i_p + s_p, j_q])
    dist = j_q[:, None] - j_k[None, :]
    return np.where((dist >= 0) & (dist <= WINDOW_REACH), 0.0, MASKED).astype(np.float32)


def _attn_kernel(q_ref, k_ref, v_ref, b0_ref, b1_ref, b2_ref, o_ref,
                 num_ref, m_ref, l_ref, nat_ref, s0_ref, p0_ref, s1_ref, p1_ref, *, run_len):
    bias_refs = (b0_ref, b1_ref, b2_ref)
    stage_refs = ((s0_ref, p0_ref), (s1_ref, p1_ref), (s1_ref, p1_ref))

    def rows_of(ref, starts, size):
        parts = [ref[st:st + size, :] for st in starts]
        return parts[0] if len(parts) == 1 else jnp.concatenate(parts, axis=0)

    def keys_of(ref, branch, starts, with_prev):
        n_runs, width, back = BRANCHES[branch]
        cur = rows_of(ref, starts, width)
        if not with_prev:
            return cur
        return jnp.concatenate([rows_of(ref, [st - back for st in starts], back), cur], axis=0)

    def layout(branch, blocks, slot):
        n_runs, width, back = BRANCHES[branch]
        n_cls = MAX_DILATION // n_runs
        out = []
        for u, (cls, i0, with_prev) in enumerate(blocks):
            starts = [(cls + n_cls * s) * run_len + i0 for s in range(n_runs)]
            out.append((slot * ATT_GROUP[branch] + u, starts, with_prev,
                        n_runs * (width + (back if with_prev else 0))))
        return out

    def issue(branch, blocks, slot):
        n_runs, width, back = BRANCHES[branch]
        s_ref, _ = stage_refs[branch]
        for idx, starts, with_prev, n_keys in layout(branch, blocks, slot):
            q = rows_of(q_ref, starts, width)
            k = keys_of(k_ref, branch, starts, with_prev)
            bias = bias_refs[branch][...] if with_prev else bias_refs[branch][:, n_runs * back:]
            s_ref[idx, :, 0:n_keys] = lax.dot_general(
                q, k, (((1,), (1,)), ((), ())), preferred_element_type=F32) + bias

    def finish(branch, blocks, slot):
        n_runs, width, back = BRANCHES[branch]
        s_ref, p_ref = stage_refs[branch]
        plan = layout(branch, blocks, slot)
        for idx, starts, with_prev, n_keys in plan:
            s = s_ref[idx, :, 0:n_keys]
            m = jnp.max(s, axis=-1, keepdims=True)
            p_ref[idx, :, 0:n_keys] = jnp.exp(s - m).astype(BF16)
            for part, st in enumerate(starts):
                m_ref[branch, st:st + width, :] = jnp.broadcast_to(
                    m[part * width:(part + 1) * width], (width, ATT_HEAD_DIM))
        for idx, starts, with_prev, n_keys in plan:
            v = keys_of(v_ref, branch, starts, with_prev)
            v1 = jnp.concatenate([v, jnp.ones((n_keys, ATT_HEAD_DIM), BF16)], axis=1)
            pv = jnp.dot(p_ref[idx, :, 0:n_keys], v1, preferred_element_type=F32)
            for part, st in enumerate(starts):
                rows = slice(part * width, (part + 1) * width)
                num_ref[branch, st:st + width, :] = pv[rows, 0:ATT_HEAD_DIM]
                l_ref[branch, st:st + width, :] = pv[rows, ATT_HEAD_DIM:]

    groups = []
    for branch, (n_runs, width, back) in enumerate(BRANCHES):
        blocks = [(cls, i0, i0 > 0) for cls in range(MAX_DILATION // n_runs) for i0 in range(0, run_len, width)]
        size = ATT_GROUP[branch]
        groups += [(branch, blocks[g:g + size]) for g in range(0, len(blocks), size)]
    issue(*groups[0], 0)
    for n, group in enumerate(groups):
        if n + 1 < len(groups):
            issue(*groups[n + 1], (n + 1) % 2)
        finish(*group, n % 2)

    def combine(r, carry):
        sl = pl.ds(pl.multiple_of(r * run_len, run_len), run_len)
        m0, m1, m2 = m_ref[0, sl, :], m_ref[1, sl, :], m_ref[2, sl, :]
        top = jnp.maximum(jnp.maximum(m0, m1), m2)
        w0, w1, w2 = jnp.exp(m0 - top), jnp.exp(m1 - top), jnp.exp(m2 - top)
        den = w0 * l_ref[0, sl, :] + w1 * l_ref[1, sl, :] + w2 * l_ref[2, sl, :]
        num = w0 * num_ref[0, sl, :] + w1 * num_ref[1, sl, :] + w2 * num_ref[2, sl, :]
        nat_ref[pl.ds(r, run_len, stride=MAX_DILATION), :] = num / den
        return carry
    lax.fori_loop(0, MAX_DILATION, combine, 0)

    rows = 256

    def emit(c, carry):
        sl = pl.ds(pl.multiple_of(c * rows, rows), rows)
        o_ref[sl, :] = nat_ref[sl, :].astype(BF16)
        return carry
    lax.fori_loop(0, (MAX_DILATION * run_len) // rows, emit, 0)


def _stage_scratch():
    out = []
    for branch in (0, 1):
        n_runs, width, back = BRANCHES[branch]
        shape = (2 * ATT_GROUP[branch], n_runs * width, n_runs * (width + back))
        out += [pltpu.VMEM(shape, F32), pltpu.VMEM(shape, BF16)]
    return out


def _attention(qkv, batch, seq):
    run_len = seq // MAX_DILATION
    qkv2 = qkv.reshape(batch, seq, 3 * D_ATT)
    biases = [jnp.asarray(_branch_bias(*b)) for b in BRANCHES]

    def head_block(offset):
        return pl.BlockSpec((None, seq, ATT_HEAD_DIM), lambda b, h: (b, 0, offset + h))

    def full(shape):
        return pl.BlockSpec(shape, lambda b, h: (0,) * len(shape))

    return pl.pallas_call(
        functools.partial(_attn_kernel, run_len=run_len),
        grid=(batch, ATT_HEADS),
        in_specs=[head_block(0), head_block(ATT_HEADS), head_block(2 * ATT_HEADS)]
        + [full(b.shape) for b in biases],
        out_specs=pl.BlockSpec((None, seq, ATT_HEAD_DIM), lambda b, h: (b, 0, h)),
        out_shape=jax.ShapeDtypeStruct((batch, seq, D_ATT), BF16),
        scratch_shapes=[
            pltpu.VMEM((len(BRANCHES), seq, ATT_HEAD_DIM), F32),
            pltpu.VMEM((len(BRANCHES), seq, ATT_HEAD_DIM), F32),
            pltpu.VMEM((len(BRANCHES), seq, ATT_HEAD_DIM), F32),
            pltpu.VMEM((seq, ATT_HEAD_DIM), F32),
        ] + _stage_scratch(),
        compiler_params=_params(("arbitrary", "arbitrary")),
        name="dilated_attn",
    )(qkv2, qkv2, qkv2, *biases)


OUT_TM = 512


def _outproj_kernel(ys_ref, ya_ref, w_ref, x_ref, gpost_ref, gpre_ref, h_ref, u_ref, mix_ref):
    mix_ref[...] = jnp.dot(ys_ref[...], w_ref[0:D_SSM, :], preferred_element_type=F32)
    mix_ref[...] += jnp.dot(ya_ref[...], w_ref[D_SSM:D_MIX, :], preferred_element_type=F32)
    chunk = 128

    def body(c, carry):
        sl = pl.ds(pl.multiple_of(c * chunk, chunk), chunk)
        h = x_ref[sl, :] + _rms_scale(mix_ref[sl, :], gpost_ref[...])
        h_ref[sl, :] = h
        u_ref[sl, :] = _rms_scale(h, gpre_ref[...]).astype(BF16)
        return carry
    lax.fori_loop(0, OUT_TM // chunk, body, 0)


def _out_projection(y_ssm, y_att, w_out, x2, g_post, g_pre):
    tokens = x2.shape[0]

    def rows(width):
        return pl.BlockSpec((OUT_TM, width), lambda i: (i, 0))

    def whole(shape):
        return pl.BlockSpec(shape, lambda i: (0, 0), pipeline_mode=pl.Buffered(1))

    return pl.pallas_call(
        _outproj_kernel,
        grid=(tokens // OUT_TM,),
        in_specs=[rows(D_SSM), rows(D_ATT), whole((D_MIX, D_MODEL)), rows(D_MODEL),
                  whole((1, D_MODEL)), whole((1, D_MODEL))],
        out_specs=[rows(D_MODEL), rows(D_MODEL)],
        out_shape=[
            jax.ShapeDtypeStruct((tokens, D_MODEL), F32),
            jax.ShapeDtypeStruct((tokens, D_MODEL), BF16),
        ],
        scratch_shapes=[pltpu.VMEM((OUT_TM, D_MODEL), F32)],
        compiler_params=_params(("arbitrary",)),
        name="outproj",
    )(y_ssm, y_att, w_out, x2, g_post, g_pre)


MLP_TM = 512
MLP_TF = 1024


def _mlp_kernel(u_ref, wup_ref, wdown_ref, h_ref, g_ref, o_ref, acc_ref):
    f = pl.program_id(1)

    @pl.when(f == 0)
    def _():
        acc_ref[...] = jnp.zeros_like(acc_ref)

    hid = jnp.maximum(jnp.dot(u_ref[...], wup_ref[...], preferred_element_type=F32), 0.0)
    acc_ref[...] += jnp.dot((hid * hid).astype(BF16), wdown_ref[...], preferred_element_type=F32)

    @pl.when(f == pl.num_programs(1) - 1)
    def _():
        chunk = 128

        def body(c, carry):
            sl = pl.ds(pl.multiple_of(c * chunk, chunk), chunk)
            o_ref[sl, :] = h_ref[sl, :] + _rms_scale(acc_ref[sl, :], g_ref[...])
            return carry
        lax.fori_loop(0, MLP_TM // chunk, body, 0)


def _mlp(u2, w_up, w_down, h1, g_post):
    tokens = u2.shape[0]
    return pl.pallas_call(
        _mlp_kernel,
        grid=(tokens // MLP_TM, D_FF // MLP_TF),
        in_specs=[
            pl.BlockSpec((MLP_TM, D_MODEL), lambda i, f: (i, 0)),
            pl.BlockSpec((D_MODEL, MLP_TF), lambda i, f: (0, f)),
            pl.BlockSpec((MLP_TF, D_MODEL), lambda i, f: (f, 0)),
            pl.BlockSpec((MLP_TM, D_MODEL), lambda i, f: (i, 0)),
            pl.BlockSpec((1, D_MODEL), lambda i, f: (0, 0)),
        ],
        out_specs=pl.BlockSpec((MLP_TM, D_MODEL), lambda i, f: (i, 0)),
        out_shape=jax.ShapeDtypeStruct((tokens, D_MODEL), F32),
        scratch_shapes=[pltpu.VMEM((MLP_TM, D_MODEL), F32)],
        compiler_params=_params(("arbitrary", "arbitrary")),
        name="mlp",
    )(u2, w_up, w_down, h1, g_post)


def kernel(x, norm_mix_pre, w_in, conv_w, conv_b, dt_bias, a_log, d_skip, ssm_norm_w, w_out,
           norm_mix_post, norm_mlp_pre, w_up, w_down, norm_mlp_post):
    batch, seq, _ = x.shape
    depth = w_in.shape[0]
    ssm_cols = D_SSM + D_SSM + D_BC
    h = x.reshape(batch * seq, D_MODEL)
    for layer in range(depth):
        w = w_in[layer]
        w_main = jnp.concatenate([w[:, :ssm_cols], w[:, ssm_cols + SSM_HEADS:]], axis=1).astype(BF16)
        w_dt = jnp.pad(w[:, ssm_cols:ssm_cols + SSM_HEADS], ((0, 0), (0, LANES - SSM_HEADS))).astype(BF16)
        zxbc, dt_raw, qkv = _in_projection(
            h, norm_mix_pre[layer].reshape(1, D_MODEL), w_main, w_dt, batch, seq)
        y_ssm = _ssd(zxbc, dt_raw, conv_w[layer], conv_b[layer], dt_bias[layer], a_log[layer],
                     d_skip[layer], ssm_norm_w[layer], batch, seq)
        y_att = _attention(qkv, batch, seq).reshape(batch * seq, D_ATT)
        h1, u2 = _out_projection(
            y_ssm, y_att, w_out[layer].astype(BF16), h,
            norm_mix_post[layer].reshape(1, D_MODEL), norm_mlp_pre[layer].reshape(1, D_MODEL))
        h = _mlp(u2, w_up[layer].astype(BF16), w_down[layer].astype(BF16), h1,
                 norm_mlp_post[layer].reshape(1, D_MODEL))
    return h.reshape(batch, seq, D_MODEL)
```

```python
import functools

import numpy as np
import jax
import jax.numpy as jnp
from jax import lax
from jax.experimental import pallas as pl
from jax.experimental.pallas import tpu as pltpu

F32 = jnp.float32
BF16 = jnp.bfloat16

D_MODEL = 2048
SSM_HEAD_DIM = 64
SSM_HEADS = 32
SSM_GROUPS = 8
HEADS_PER_GROUP = 4
D_STATE = 128
D_SSM = SSM_HEADS * SSM_HEAD_DIM
D_BC = 2 * SSM_GROUPS * D_STATE
GROUP_WIDTH = HEADS_PER_GROUP * SSM_HEAD_DIM
CONV_WIDTH = 4
CHUNK = 128
ATT_HEADS = 16
ATT_HEAD_DIM = 128
D_ATT = ATT_HEADS * ATT_HEAD_DIM
D_MIX = D_SSM + D_ATT
D_FF = 4 * D_MODEL
WINDOW_REACH = 128
MAX_DILATION = 16
EPS = 1e-6
ATT_SCALE = ATT_HEAD_DIM ** -0.5
MASKED = -1e30

LANES = 128
VMEM_LIMIT = 56 * 1024 * 1024


def _params(semantics):
    return pltpu.CompilerParams(dimension_semantics=semantics, vmem_limit_bytes=VMEM_LIMIT)


def _silu(v):
    return v * (1.0 / (1.0 + jnp.exp(-v)))


def _split3(v):
    hi = v.astype(BF16)
    rest = v - hi.astype(F32)
    mid = rest.astype(BF16)
    lo = (rest - mid.astype(F32)).astype(BF16)
    return hi, mid, lo


def _rms_scale(v, gain):
    ms = jnp.mean(v * v, axis=-1, keepdims=True)
    return (v * lax.rsqrt(ms + EPS)) * gain


PREP_ROWS = 256


def _prep_kernel(w_ref, main_ref, dt_ref):
    ssm_cols = 2 * D_SSM + D_BC
    att0 = ssm_cols + SSM_HEADS
    main_ref[:, 0:ssm_cols] = w_ref[:, 0:ssm_cols].astype(BF16)
    main_ref[:, ssm_cols:ssm_cols + 3 * D_ATT] = w_ref[:, att0:att0 + 3 * D_ATT].astype(BF16)
    dt_cols = w_ref[:, ssm_cols:ssm_cols + SSM_HEADS]
    dt_ref[...] = jnp.concatenate(
        [dt_cols, jnp.zeros((PREP_ROWS, LANES - SSM_HEADS), F32)], axis=1).astype(BF16)


def _prep_in_weights(w):
    d_in = w.shape[1]
    main_cols = 2 * D_SSM + D_BC + 3 * D_ATT
    return pl.pallas_call(
        _prep_kernel,
        grid=(D_MODEL // PREP_ROWS,),
        in_specs=[pl.BlockSpec((PREP_ROWS, d_in), lambda i: (i, 0))],
        out_specs=[pl.BlockSpec((PREP_ROWS, main_cols), lambda i: (i, 0)),
                   pl.BlockSpec((PREP_ROWS, LANES), lambda i: (i, 0))],
        out_shape=[jax.ShapeDtypeStruct((D_MODEL, main_cols), BF16),
                   jax.ShapeDtypeStruct((D_MODEL, LANES), BF16)],
        compiler_params=_params(("arbitrary",)),
        name="inproj_weights",
    )(w)


IN_TM = 1024
IN_TN = 1024
IN_STAGE_SLABS = 4
IN_STAGE_PITCH = 24


def _inproj_kernel(x_ref, g_ref, w_ref, wdt_ref, zxbc_ref, dt_ref, qkv_ref, u_ref, uperm_ref, uf_ref,
                   *, n_ssm_tiles):
    j = pl.program_id(1)
    chunk = 128
    n_slabs = D_MODEL // LANES
    rows = IN_TM // MAX_DILATION

    @pl.when(j == 0)
    def _():
        def body(c, carry):
            sl = pl.ds(pl.multiple_of(c * chunk, chunk), chunk)
            u_ref[sl, :] = _rms_scale(x_ref[sl, :], g_ref[...]).astype(BF16)
            return carry
        lax.fori_loop(0, IN_TM // chunk, body, 0)
        dt_ref[...] = jnp.dot(u_ref[...], wdt_ref[...], preferred_element_type=F32)
        for s0 in range(0, n_slabs, IN_STAGE_SLABS):
            for s in range(IN_STAGE_SLABS):
                cols = slice((s0 + s) * LANES, (s0 + s + 1) * LANES)
                for grp in range(rows):
                    uf_ref[s, grp * IN_STAGE_PITCH:grp * IN_STAGE_PITCH + MAX_DILATION, :] = (
                        u_ref[grp * MAX_DILATION:(grp + 1) * MAX_DILATION, cols].astype(F32))
            for r in range(MAX_DILATION):
                for s in range(IN_STAGE_SLABS):
                    uperm_ref[r * rows:(r + 1) * rows, (s0 + s) * LANES:(s0 + s + 1) * LANES] = (
                        uf_ref[s, pl.ds(r, rows, stride=IN_STAGE_PITCH), :].astype(BF16))

    @pl.when(j < n_ssm_tiles)
    def _():
        zxbc_ref[...] = jnp.dot(u_ref[...], w_ref[...], preferred_element_type=F32)

    @pl.when(j >= n_ssm_tiles)
    def _():
        scale = jnp.where(j < n_ssm_tiles + D_ATT // IN_TN, ATT_SCALE, 1.0).astype(F32)
        res = jnp.dot(uperm_ref[...], w_ref[...], preferred_element_type=F32) * scale
        for r in range(MAX_DILATION):
            qkv_ref[r] = res[r * rows:(r + 1) * rows, :].astype(BF16)


def _in_projection(x2, gain, w_main, w_dt, batch, seq):
    tokens = x2.shape[0]
    n_m = tokens // IN_TM
    m_per_seq = seq // IN_TM
    ssm_cols = 2 * D_SSM + D_BC
    att_cols = 3 * D_ATT
    n_ssm_tiles = ssm_cols // IN_TN
    n_att_tiles = att_cols // IN_TN
    run_len = seq // MAX_DILATION
    return pl.pallas_call(
        functools.partial(_inproj_kernel, n_ssm_tiles=n_ssm_tiles),
        grid=(n_m, n_ssm_tiles + n_att_tiles),
        in_specs=[
            pl.BlockSpec((IN_TM, D_MODEL), lambda i, j: (i, 0)),
            pl.BlockSpec((1, D_MODEL), lambda i, j: (0, 0)),
            pl.BlockSpec((D_MODEL, IN_TN), lambda i, j: (0, j)),
            pl.BlockSpec((D_MODEL, LANES), lambda i, j: (0, 0)),
        ],
        out_specs=[
            pl.BlockSpec((IN_TM, IN_TN), lambda i, j: (i, jnp.minimum(j, n_ssm_tiles - 1))),
            pl.BlockSpec((IN_TM, LANES), lambda i, j: (i, 0)),
            pl.BlockSpec((None, MAX_DILATION, IN_TM // MAX_DILATION, IN_TN),
                         lambda i, j: (i // m_per_seq, 0, i % m_per_seq, jnp.maximum(j - n_ssm_tiles, 0))),
        ],
        out_shape=[
            jax.ShapeDtypeStruct((tokens, ssm_cols), F32),
            jax.ShapeDtypeStruct((tokens, LANES), F32),
            jax.ShapeDtypeStruct((batch, MAX_DILATION, run_len, att_cols), BF16),
        ],
        scratch_shapes=[
            pltpu.VMEM((IN_TM, D_MODEL), BF16),
            pltpu.VMEM((IN_TM, D_MODEL), BF16),
            pltpu.VMEM((IN_STAGE_SLABS, IN_TM // MAX_DILATION * IN_STAGE_PITCH, LANES), F32),
        ],
        compiler_params=_params(("arbitrary", "arbitrary")),
        name="inproj",
    )(x2, gain, w_main, w_dt)


CONV_PAD = 8


def _ssd_kernel(z_ref, x_ref, bc_ref, dt_ref, cwx_ref, cbx_ref, cwbc_ref, cbbc_ref,
                dtb_ref, alog_ref, dsk_ref, nw_ref, tri_ref, expand_ref,
                y_ref, state_ref, xpad_ref, bcpad_ref):
    @pl.when(pl.program_id(1) == 0)
    def _():
        state_ref[...] = jnp.zeros_like(state_ref)
        xpad_ref[0:CONV_PAD, :] = jnp.zeros((CONV_PAD, D_SSM), F32)
        bcpad_ref[0:CONV_PAD, :] = jnp.zeros((CONV_PAD, D_BC), F32)

    xpad_ref[CONV_PAD:CONV_PAD + CHUNK, :] = x_ref[...]
    bcpad_ref[CONV_PAD:CONV_PAD + CHUNK, :] = bc_ref[...]

    def conv_silu(pad_ref, w_ref, b_ref, c0, width):
        cols = slice(c0, c0 + width)
        acc = b_ref[:, cols]
        for k in range(CONV_WIDTH):
            start = CONV_PAD - (CONV_WIDTH - 1) + k
            acc = acc + w_ref[k:k + 1, cols] * pad_ref[start:start + CHUNK, cols]
        return _silu(acc)

    dt_in = dt_ref[...] + dtb_ref[...]
    dtv = jnp.maximum(dt_in, 0.0) + jnp.log1p(jnp.exp(-jnp.abs(dt_in)))
    da = dtv * (-jnp.exp(alog_ref[...]))
    tri = tri_ref[...]
    a_cs = sum(jnp.dot(tri, part, preferred_element_type=F32)
               for part in _split3(da))
    a_last = a_cs[CHUNK - 1:CHUNK, :]
    w_end = dtv * jnp.exp(a_last - a_cs)
    a_cs_t = a_cs.T
    dt_t = dtv.T
    w_end_t = w_end.T
    expand = expand_ref[...]
    chunk_decay = sum(jnp.dot(part, expand, preferred_element_type=F32)
                      for part in _split3(jnp.broadcast_to(jnp.exp(a_last), (8, LANES)))
                      )[0:1, :]

    row_i = lax.broadcasted_iota(jnp.int32, (CHUNK, CHUNK), 0)
    col_j = lax.broadcasted_iota(jnp.int32, (CHUNK, CHUNK), 1)
    causal = row_i >= col_j
    lane_head = lax.broadcasted_iota(jnp.int32, (CHUNK, GROUP_WIDTH), 1) // SSM_HEAD_DIM
    own_lanes = [jnp.where(lane_head == r, 1.0, 0.0).astype(BF16) for r in range(HEADS_PER_GROUP)]

    for g in range(SSM_GROUPS):
        gcols = slice(g * GROUP_WIDTH, (g + 1) * GROUP_WIDTH)
        xs = conv_silu(xpad_ref, cwx_ref, cbx_ref, g * GROUP_WIDTH, GROUP_WIDTH)
        bm = conv_silu(bcpad_ref, cwbc_ref, cbbc_ref, g * D_STATE, D_STATE)
        cm = conv_silu(bcpad_ref, cwbc_ref, cbbc_ref, D_BC // 2 + g * D_STATE, D_STATE)
        cb = lax.dot_general(cm.astype(BF16), bm.astype(BF16), (((1,), (1,)), ((), ())),
                             preferred_element_type=F32)
        bm_t = bm.T
        prev = state_ref[g]
        xs16 = xs.astype(BF16)
        prev16 = prev.astype(BF16)

        lhs_y, rhs_y, lhs_s, rhs_s = [], [], [], []
        for r in range(HEADS_PER_GROUP):
            h = g * HEADS_PER_GROUP + r
            col_a = jnp.broadcast_to(a_cs[:, h:h + 1], (CHUNK, CHUNK))
            row_a = a_cs_t[h:h + 1, :]
            decay = jnp.exp(jnp.where(causal, col_a - row_a, -jnp.inf))
            lhs_y.append((cb * decay * dt_t[h:h + 1, :]).astype(BF16))
            lhs_y.append((cm * jnp.exp(col_a)).astype(BF16))
            x_own = xs16 * own_lanes[r]
            rhs_y.append(x_own)
            rhs_y.append(prev16 * own_lanes[r])
            lhs_s.append((bm_t * w_end_t[h:h + 1, :]).astype(BF16))
            rhs_s.append(x_own)

        y = jnp.dot(jnp.concatenate(lhs_y, axis=1), jnp.concatenate(rhs_y, axis=0),
                    preferred_element_type=F32)
        s_new = jnp.dot(jnp.concatenate(lhs_s, axis=1), jnp.concatenate(rhs_s, axis=0),
                        preferred_element_type=F32)
        state_ref[g] = prev * chunk_decay[:, gcols] + s_new

        y = y + dsk_ref[:, gcols] * xs
        gated = y * _silu(z_ref[:, gcols])
        y_ref[:, gcols] = _rms_scale(gated, nw_ref[:, gcols]).astype(BF16)

    xpad_ref[0:CONV_PAD, :] = x_ref[CHUNK - CONV_PAD:CHUNK, :]
    bcpad_ref[0:CONV_PAD, :] = bc_ref[CHUNK - CONV_PAD:CHUNK, :]


def _ssd(zxbc, dt_raw, conv_w, conv_b, dt_bias, a_log, d_skip, ssm_norm_w, batch, seq):
    tokens = zxbc.shape[0]
    n_chunks = seq // CHUNK
    pad_heads = LANES - SSM_HEADS
    dtb = jnp.pad(dt_bias.astype(F32), (0, pad_heads)).reshape(1, LANES)
    alog = jnp.pad(a_log.astype(F32), (0, pad_heads)).reshape(1, LANES)
    dsk = jnp.repeat(d_skip.astype(F32), SSM_HEAD_DIM).reshape(1, D_SSM)
    tri = jnp.asarray(np.tril(np.ones((CHUNK, CHUNK), np.float32)), dtype=BF16)
    expand = np.zeros((LANES, D_SSM), np.float32)
    for h in range(SSM_HEADS):
        expand[h, h * SSM_HEAD_DIM:(h + 1) * SSM_HEAD_DIM] = 1.0
    expand = jnp.asarray(expand, dtype=BF16)

    def rows(b, c):
        return b * n_chunks + c

    def full(shape):
        return pl.BlockSpec(shape, lambda b, c: (0,) * len(shape))

    return pl.pallas_call(
        _ssd_kernel,
        grid=(batch, n_chunks),
        in_specs=[
            pl.BlockSpec((CHUNK, D_SSM), lambda b, c: (rows(b, c), 0)),
            pl.BlockSpec((CHUNK, D_SSM), lambda b, c: (rows(b, c), 1)),
            pl.BlockSpec((CHUNK, D_BC), lambda b, c: (rows(b, c), 2)),
            pl.BlockSpec((CHUNK, LANES), lambda b, c: (rows(b, c), 0)),
            full((CONV_WIDTH, D_SSM)), full((1, D_SSM)),
            full((CONV_WIDTH, D_BC)), full((1, D_BC)),
            full((1, LANES)), full((1, LANES)), full((1, D_SSM)), full((1, D_SSM)),
            full((CHUNK, CHUNK)), full((LANES, D_SSM)),
        ],
        out_specs=pl.BlockSpec((CHUNK, D_SSM), lambda b, c: (rows(b, c), 0)),
        out_shape=jax.ShapeDtypeStruct((tokens, D_SSM), BF16),
        scratch_shapes=[
            pltpu.VMEM((SSM_GROUPS, D_STATE, GROUP_WIDTH), F32),
            pltpu.VMEM((CONV_PAD + CHUNK, D_SSM), F32),
            pltpu.VMEM((CONV_PAD + CHUNK, D_BC), F32),
        ],
        compiler_params=_params(("arbitrary", "arbitrary")),
        name="ssd",
    )(zxbc, zxbc, zxbc, dt_raw,
      conv_w[:, :D_SSM], conv_b[:D_SSM].reshape(1, D_SSM),
      conv_w[:, D_SSM:], conv_b[D_SSM:].reshape(1, D_BC),
      dtb, alog, dsk, ssm_norm_w.reshape(1, D_SSM), tri, expand)


BRANCHES = ((16, 16, 16), (4, 32, 32), (1, 128, 128))
ATT_GROUP = (4, 8, 8)


def _branch_bias(n_runs, width, back):
    s_q = np.repeat(np.arange(n_runs), width)
    i_q = np.tile(np.arange(width), n_runs)
    j_q = n_runs * i_q + s_q
    s_p = np.repeat(np.arange(n_runs), back)
    i_p = np.tile(np.arange(back), n_runs) - back
    j_k = np.concatenate([n_runs * i_p + s_p, j_q])
    dist = j_q[:, None] - j_k[None, :]
    return np.where((dist >= 0) & (dist <= WINDOW_REACH), 0.0, MASKED).astype(np.float32)


def _attn_kernel(q_ref, k_ref, v_ref, b0_ref, b1_ref, b2_ref, o_ref,
                 num_ref, m_ref, l_ref, nat_ref, s0_ref, p0_ref, s1_ref, p1_ref, *, run_len):
    bias_refs = (b0_ref, b1_ref, b2_ref)
    stage_refs = ((s0_ref, p0_ref), (s1_ref, p1_ref), (s1_ref, p1_ref))

    def rows_of(ref, starts, size):
        parts = [ref[st:st + size, :] for st in starts]
        return parts[0] if len(parts) == 1 else jnp.concatenate(parts, axis=0)

    def keys_of(ref, branch, starts, with_prev):
        n_runs, width, back = BRANCHES[branch]
        cur = rows_of(ref, starts, width)
        if not with_prev:
            return cur
        return jnp.concatenate([rows_of(ref, [st - back for st in starts], back), cur], axis=0)

    def layout(branch, blocks, slot):
        n_runs, width, back = BRANCHES[branch]
        n_cls = MAX_DILATION // n_runs
        out = []
        for u, (cls, i0, with_prev) in enumerate(blocks):
            starts = [(cls + n_cls * s) * run_len + i0 for s in range(n_runs)]
            out.append((slot * ATT_GROUP[branch] + u, starts, with_prev,
                        n_runs * (width + (back if with_prev else 0))))
        return out

    def issue(branch, blocks, slot):
        n_runs, width, back = BRANCHES[branch]
        s_ref, _ = stage_refs[branch]
        for idx, starts, with_prev, n_keys in layout(branch, blocks, slot):
            q = rows_of(q_ref, starts, width)
            k = keys_of(k_ref, branch, starts, with_prev)
            bias = bias_refs[branch][...] if with_prev else bias_refs[branch][:, n_runs * back:]
            s_ref[idx, :, 0:n_keys] = lax.dot_general(
                q, k, (((1,), (1,)), ((), ())), preferred_element_type=F32) + bias

    def finish(branch, blocks, slot):
        n_runs, width, back = BRANCHES[branch]
        s_ref, p_ref = stage_refs[branch]
        plan = layout(branch, blocks, slot)
        for idx, starts, with_prev, n_keys in plan:
            s = s_ref[idx, :, 0:n_keys]
            m = jnp.max(s, axis=-1, keepdims=True)
            p_ref[idx, :, 0:n_keys] = jnp.exp(s - m).astype(BF16)
            for part, st in enumerate(starts):
                m_ref[branch, st:st + width, :] = jnp.broadcast_to(
                    m[part * width:(part + 1) * width], (width, ATT_HEAD_DIM))
        for idx, starts, with_prev, n_keys in plan:
            v = keys_of(v_ref, branch, starts, with_prev)
            v1 = jnp.concatenate([v, jnp.ones((n_keys, ATT_HEAD_DIM), BF16)], axis=1)
            pv = jnp.dot(p_ref[idx, :, 0:n_keys], v1, preferred_element_type=F32)
            for part, st in enumerate(starts):
                rows = slice(part * width, (part + 1) * width)
                num_ref[branch, st:st + width, :] = pv[rows, 0:ATT_HEAD_DIM]
                l_ref[branch, st:st + width, :] = pv[rows, ATT_HEAD_DIM:]

    groups = []
    for branch, (n_runs, width, back) in enumerate(BRANCHES):
        blocks = [(cls, i0, i0 > 0) for cls in range(MAX_DILATION // n_runs) for i0 in range(0, run_len, width)]
        size = ATT_GROUP[branch]
        groups += [(branch, blocks[g:g + size]) for g in range(0, len(blocks), size)]
    issue(*groups[0], 0)
    for n, group in enumerate(groups):
        if n + 1 < len(groups):
            issue(*groups[n + 1], (n + 1) % 2)
        finish(*group, n % 2)

    def combine(r, carry):
        sl = pl.ds(pl.multiple_of(r * run_len, run_len), run_len)
        m0, m1, m2 = m_ref[0, sl, :], m_ref[1, sl, :], m_ref[2, sl, :]
        top = jnp.maximum(jnp.maximum(m0, m1), m2)
        w0, w1, w2 = jnp.exp(m0 - top), jnp.exp(m1 - top), jnp.exp(m2 - top)
        den = w0 * l_ref[0, sl, :] + w1 * l_ref[1, sl, :] + w2 * l_ref[2, sl, :]
        num = w0 * num_ref[0, sl, :] + w1 * num_ref[1, sl, :] + w2 * num_ref[2, sl, :]
        nat_ref[pl.ds(r, run_len, stride=MAX_DILATION), :] = num / den
        return carry
    lax.fori_loop(0, MAX_DILATION, combine, 0)

    rows = 256

    def emit(c, carry):
        sl = pl.ds(pl.multiple_of(c * rows, rows), rows)
        o_ref[sl, :] = nat_ref[sl, :].astype(BF16)
        return carry
    lax.fori_loop(0, (MAX_DILATION * run_len) // rows, emit, 0)


def _stage_scratch():
    out = []
    for branch in (0, 1):
        n_runs, width, back = BRANCHES[branch]
        shape = (2 * ATT_GROUP[branch], n_runs * width, n_runs * (width + back))
        out += [pltpu.VMEM(shape, F32), pltpu.VMEM(shape, BF16)]
    return out


def _attention(qkv, batch, seq):
    run_len = seq // MAX_DILATION
    qkv2 = qkv.reshape(batch, seq, 3 * D_ATT)
    biases = [jnp.asarray(_branch_bias(*b)) for b in BRANCHES]

    def head_block(offset):
        return pl.BlockSpec((None, seq, ATT_HEAD_DIM), lambda b, h: (b, 0, offset + h))

    def full(shape):
        return pl.BlockSpec(shape, lambda b, h: (0,) * len(shape))

    return pl.pallas_call(
        functools.partial(_attn_kernel, run_len=run_len),
        grid=(batch, ATT_HEADS),
        in_specs=[head_block(0), head_block(ATT_HEADS), head_block(2 * ATT_HEADS)]
        + [full(b.shape) for b in biases],
        out_specs=pl.BlockSpec((None, seq, ATT_HEAD_DIM), lambda b, h: (b, 0, h)),
        out_shape=jax.ShapeDtypeStruct((batch, seq, D_ATT), BF16),
        scratch_shapes=[
            pltpu.VMEM((len(BRANCHES), seq, ATT_HEAD_DIM), F32),
            pltpu.VMEM((len(BRANCHES), seq, ATT_HEAD_DIM), F32),
            pltpu.VMEM((len(BRANCHES), seq, ATT_HEAD_DIM), F32),
            pltpu.VMEM((seq, ATT_HEAD_DIM), F32),
        ] + _stage_scratch(),
        compiler_params=_params(("arbitrary", "arbitrary")),
        name="dilated_attn",
    )(qkv2, qkv2, qkv2, *biases)


OUT_TM = 512


def _outproj_kernel(ys_ref, ya_ref, w_ref, x_ref, gpost_ref, gpre_ref, h_ref, u_ref, mix_ref):
    mix_ref[...] = jnp.dot(ys_ref[...], w_ref[0:D_SSM, :], preferred_element_type=F32)
    mix_ref[...] += jnp.dot(ya_ref[...], w_ref[D_SSM:D_MIX, :], preferred_element_type=F32)
    chunk = 128

    def body(c, carry):
        sl = pl.ds(pl.multiple_of(c * chunk, chunk), chunk)
        h = x_ref[sl, :] + _rms_scale(mix_ref[sl, :], gpost_ref[...])
        h_ref[sl, :] = h
        u_ref[sl, :] = _rms_scale(h, gpre_ref[...]).astype(BF16)
        return carry
    lax.fori_loop(0, OUT_TM // chunk, body, 0)


def _out_projection(y_ssm, y_att, w_out, x2, g_post, g_pre):
    tokens = x2.shape[0]

    def rows(width):
        return pl.BlockSpec((OUT_TM, width), lambda i: (i, 0))

    def whole(shape):
        return pl.BlockSpec(shape, lambda i: (0, 0), pipeline_mode=pl.Buffered(1))

    return pl.pallas_call(
        _outproj_kernel,
        grid=(tokens // OUT_TM,),
        in_specs=[rows(D_SSM), rows(D_ATT), whole((D_MIX, D_MODEL)), rows(D_MODEL),
                  whole((1, D_MODEL)), whole((1, D_MODEL))],
        out_specs=[rows(D_MODEL), rows(D_MODEL)],
        out_shape=[
            jax.ShapeDtypeStruct((tokens, D_MODEL), F32),
            jax.ShapeDtypeStruct((tokens, D_MODEL), BF16),
        ],
        scratch_shapes=[pltpu.VMEM((OUT_TM, D_MODEL), F32)],
        compiler_params=_params(("arbitrary",)),
        name="outproj",
    )(y_ssm, y_att, w_out, x2, g_post, g_pre)


MLP_TM = 512
MLP_TF = 1024


def _mlp_kernel(u_ref, wup_ref, wdown_ref, h_ref, g_ref, o_ref, acc_ref):
    f = pl.program_id(1)

    @pl.when(f == 0)
    def _():
        acc_ref[...] = jnp.zeros_like(acc_ref)

    hid = jnp.maximum(jnp.dot(u_ref[...], wup_ref[...], preferred_element_type=F32), 0.0)
    acc_ref[...] += jnp.dot((hid * hid).astype(BF16), wdown_ref[...], preferred_element_type=F32)

    @pl.when(f == pl.num_programs(1) - 1)
    def _():
        chunk = 128

        def body(c, carry):
            sl = pl.ds(pl.multiple_of(c * chunk, chunk), chunk)
            o_ref[sl, :] = h_ref[sl, :] + _rms_scale(acc_ref[sl, :], g_ref[...])
            return carry
        lax.fori_loop(0, MLP_TM // chunk, body, 0)


def _mlp(u2, w_up, w_down, h1, g_post):
    tokens = u2.shape[0]
    return pl.pallas_call(
        _mlp_kernel,
        grid=(tokens // MLP_TM, D_FF // MLP_TF),
        in_specs=[
            pl.BlockSpec((MLP_TM, D_MODEL), lambda i, f: (i, 0)),
            pl.BlockSpec((D_MODEL, MLP_TF), lambda i, f: (0, f)),
            pl.BlockSpec((MLP_TF, D_MODEL), lambda i, f: (f, 0)),
            pl.BlockSpec((MLP_TM, D_MODEL), lambda i, f: (i, 0)),
            pl.BlockSpec((1, D_MODEL), lambda i, f: (0, 0)),
        ],
        out_specs=pl.BlockSpec((MLP_TM, D_MODEL), lambda i, f: (i, 0)),
        out_shape=jax.ShapeDtypeStruct((tokens, D_MODEL), F32),
        scratch_shapes=[pltpu.VMEM((MLP_TM, D_MODEL), F32)],
        compiler_params=_params(("arbitrary", "arbitrary")),
        name="mlp",
    )(u2, w_up, w_down, h1, g_post)


def kernel(x, norm_mix_pre, w_in, conv_w, conv_b, dt_bias, a_log, d_skip, ssm_norm_w, w_out,
           norm_mix_post, norm_mlp_pre, w_up, w_down, norm_mlp_post):
    batch, seq, _ = x.shape
    depth = w_in.shape[0]
    h = x.reshape(batch * seq, D_MODEL)
    for layer in range(depth):
        w_main, w_dt = _prep_in_weights(w_in[layer])
        zxbc, dt_raw, qkv = _in_projection(
            h, norm_mix_pre[layer].reshape(1, D_MODEL), w_main, w_dt, batch, seq)
        y_ssm = _ssd(zxbc, dt_raw, conv_w[layer], conv_b[layer], dt_bias[layer], a_log[layer],
                     d_skip[layer], ssm_norm_w[layer], batch, seq)
        y_att = _attention(qkv, batch, seq).reshape(batch * seq, D_ATT)
        h1, u2 = _out_projection(
            y_ssm, y_att, w_out[layer].astype(BF16), h,
            norm_mix_post[layer].reshape(1, D_MODEL), norm_mlp_pre[layer].reshape(1, D_MODEL))
        h = _mlp(u2, w_up[layer].astype(BF16), w_down[layer].astype(BF16), h1,
                 norm_mlp_post[layer].reshape(1, D_MODEL))
    return h.reshape(batch, seq, D_MODEL)
```

```python
import functools

import numpy as np
import jax
import jax.numpy as jnp
from jax import lax
from jax.experimental import pallas as pl
from jax.experimental.pallas import tpu as pltpu

F32 = jnp.float32
BF16 = jnp.bfloat16

D_MODEL = 2048
SSM_HEAD_DIM = 64
SSM_HEADS = 32
SSM_GROUPS = 8
HEADS_PER_GROUP = 4
D_STATE = 128
D_SSM = SSM_HEADS * SSM_HEAD_DIM
D_BC = 2 * SSM_GROUPS * D_STATE
GROUP_WIDTH = HEADS_PER_GROUP * SSM_HEAD_DIM
CONV_WIDTH = 4
CHUNK = 128
ATT_HEADS = 16
ATT_HEAD_DIM = 128
D_ATT = ATT_HEADS * ATT_HEAD_DIM
D_MIX = D_SSM + D_ATT
D_FF = 4 * D_MODEL
WINDOW_REACH = 128
MAX_DILATION = 16
EPS = 1e-6
ATT_SCALE = ATT_HEAD_DIM ** -0.5
MASKED = -1e30

LANES = 128
VMEM_LIMIT = 56 * 1024 * 1024


def _params(semantics):
    return pltpu.CompilerParams(dimension_semantics=semantics, vmem_limit_bytes=VMEM_LIMIT)


def _silu(v):
    return v * (1.0 / (1.0 + jnp.exp(-v)))


def _split3(v):
    hi = v.astype(BF16)
    rest = v - hi.astype(F32)
    mid = rest.astype(BF16)
    lo = (rest - mid.astype(F32)).astype(BF16)
    return hi, mid, lo


def _rms_scale(v, gain):
    ms = jnp.mean(v * v, axis=-1, keepdims=True)
    return (v * lax.rsqrt(ms + EPS)) * gain


PREP_COLS = 512


def _prep_kernel(wt_ref, wdt_ref, main_ref, dt_ref):
    main_ref[...] = wt_ref[...].T.astype(BF16)

    @pl.when(pl.program_id(0) == 0)
    def _():
        padded = jnp.concatenate([wdt_ref[...], jnp.zeros((LANES - SSM_HEADS, D_MODEL), F32)], axis=0)
        dt_ref[...] = padded.T.astype(BF16)


def _prep_in_weights(w):
    wt = w.T
    ssm_cols = 2 * D_SSM + D_BC
    att0 = ssm_cols + SSM_HEADS
    main_cols = ssm_cols + 3 * D_ATT
    n_ssm = ssm_cols // PREP_COLS

    def feature_row(j):
        return pl.multiple_of(jnp.where(j < n_ssm, j * PREP_COLS, att0 + (j - n_ssm) * PREP_COLS), 8)

    return pl.pallas_call(
        _prep_kernel,
        grid=(main_cols // PREP_COLS,),
        in_specs=[
            pl.BlockSpec((pl.Element(PREP_COLS), pl.Element(D_MODEL)), lambda j: (feature_row(j), 0)),
            pl.BlockSpec((pl.Element(SSM_HEADS), pl.Element(D_MODEL)), lambda j: (ssm_cols, 0)),
        ],
        out_specs=[pl.BlockSpec((D_MODEL, PREP_COLS), lambda j: (0, j)),
                   pl.BlockSpec((D_MODEL, LANES), lambda j: (0, 0))],
        out_shape=[jax.ShapeDtypeStruct((D_MODEL, main_cols), BF16),
                   jax.ShapeDtypeStruct((D_MODEL, LANES), BF16)],
        compiler_params=_params(("arbitrary",)),
        name="inproj_weights",
    )(wt, wt)


IN_TM = 1024
IN_TN = 1024
IN_STAGE_SLABS = 4
IN_STAGE_PITCH = 24


def _inproj_kernel(x_ref, g_ref, w_ref, wdt_ref, zxbc_ref, dt_ref, qkv_ref, u_ref, uperm_ref, uf_ref,
                   *, n_ssm_tiles):
    j = pl.program_id(1)
    chunk = 128
    n_slabs = D_MODEL // LANES
    rows = IN_TM // MAX_DILATION

    @pl.when(j == 0)
    def _():
        def body(c, carry):
            sl = pl.ds(pl.multiple_of(c * chunk, chunk), chunk)
            u_ref[sl, :] = _rms_scale(x_ref[sl, :], g_ref[...]).astype(BF16)
            return carry
        lax.fori_loop(0, IN_TM // chunk, body, 0)
        dt_ref[...] = jnp.dot(u_ref[...], wdt_ref[...], preferred_element_type=F32)
        for s0 in range(0, n_slabs, IN_STAGE_SLABS):
            for s in range(IN_STAGE_SLABS):
                cols = slice((s0 + s) * LANES, (s0 + s + 1) * LANES)
                for grp in range(rows):
                    uf_ref[s, grp * IN_STAGE_PITCH:grp * IN_STAGE_PITCH + MAX_DILATION, :] = (
                        u_ref[grp * MAX_DILATION:(grp + 1) * MAX_DILATION, cols].astype(F32))
            for r in range(MAX_DILATION):
                for s in range(IN_STAGE_SLABS):
                    uperm_ref[r * rows:(r + 1) * rows, (s0 + s) * LANES:(s0 + s + 1) * LANES] = (
                        uf_ref[s, pl.ds(r, rows, stride=IN_STAGE_PITCH), :].astype(BF16))

    @pl.when(j < n_ssm_tiles)
    def _():
        zxbc_ref[...] = jnp.dot(u_ref[...], w_ref[...], preferred_element_type=F32)

    @pl.when(j >= n_ssm_tiles)
    def _():
        scale = jnp.where(j < n_ssm_tiles + D_ATT // IN_TN, ATT_SCALE, 1.0).astype(F32)
        res = jnp.dot(uperm_ref[...], w_ref[...], preferred_element_type=F32) * scale
        for r in range(MAX_DILATION):
            qkv_ref[r] = res[r * rows:(r + 1) * rows, :].astype(BF16)


def _in_projection(x2, gain, w_main, w_dt, batch, seq):
    tokens = x2.shape[0]
    n_m = tokens // IN_TM
    m_per_seq = seq // IN_TM
    ssm_cols = 2 * D_SSM + D_BC
    att_cols = 3 * D_ATT
    n_ssm_tiles = ssm_cols // IN_TN
    n_att_tiles = att_cols // IN_TN
    run_len = seq // MAX_DILATION
    return pl.pallas_call(
        functools.partial(_inproj_kernel, n_ssm_tiles=n_ssm_tiles),
        grid=(n_m, n_ssm_tiles + n_att_tiles),
        in_specs=[
            pl.BlockSpec((IN_TM, D_MODEL), lambda i, j: (i, 0)),
            pl.BlockSpec((1, D_MODEL), lambda i, j: (0, 0)),
            pl.BlockSpec((D_MODEL, IN_TN), lambda i, j: (0, j)),
            pl.BlockSpec((D_MODEL, LANES), lambda i, j: (0, 0)),
        ],
        out_specs=[
            pl.BlockSpec((IN_TM, IN_TN), lambda i, j: (i, jnp.minimum(j, n_ssm_tiles - 1))),
            pl.BlockSpec((IN_TM, LANES), lambda i, j: (i, 0)),
            pl.BlockSpec((None, MAX_DILATION, IN_TM // MAX_DILATION, IN_TN),
                         lambda i, j: (i // m_per_seq, 0, i % m_per_seq, jnp.maximum(j - n_ssm_tiles, 0))),
        ],
        out_shape=[
            jax.ShapeDtypeStruct((tokens, ssm_cols), F32),
            jax.ShapeDtypeStruct((tokens, LANES), F32),
            jax.ShapeDtypeStruct((batch, MAX_DILATION, run_len, att_cols), BF16),
        ],
        scratch_shapes=[
            pltpu.VMEM((IN_TM, D_MODEL), BF16),
            pltpu.VMEM((IN_TM, D_MODEL), BF16),
            pltpu.VMEM((IN_STAGE_SLABS, IN_TM // MAX_DILATION * IN_STAGE_PITCH, LANES), F32),
        ],
        compiler_params=_params(("arbitrary", "arbitrary")),
        name="inproj",
    )(x2, gain, w_main, w_dt)


CONV_PAD = 8


def _ssd_kernel(z_ref, x_ref, bc_ref, dt_ref, cwx_ref, cbx_ref, cwbc_ref, cbbc_ref,
                dtb_ref, alog_ref, dsk_ref, nw_ref, tri_ref, expand_ref,
                y_ref, state_ref, xpad_ref, bcpad_ref):
    @pl.when(pl.program_id(1) == 0)
    def _():
        state_ref[...] = jnp.zeros_like(state_ref)
        xpad_ref[0:CONV_PAD, :] = jnp.zeros((CONV_PAD, D_SSM), F32)
        bcpad_ref[0:CONV_PAD, :] = jnp.zeros((CONV_PAD, D_BC), F32)

    xpad_ref[CONV_PAD:CONV_PAD + CHUNK, :] = x_ref[...]
    bcpad_ref[CONV_PAD:CONV_PAD + CHUNK, :] = bc_ref[...]

    def conv_silu(pad_ref, w_ref, b_ref, c0, width):
        cols = slice(c0, c0 + width)
        acc = b_ref[:, cols]
        for k in range(CONV_WIDTH):
            start = CONV_PAD - (CONV_WIDTH - 1) + k
            acc = acc + w_ref[k:k + 1, cols] * pad_ref[start:start + CHUNK, cols]
        return _silu(acc)

    dt_in = dt_ref[...] + dtb_ref[...]
    dtv = jnp.maximum(dt_in, 0.0) + jnp.log1p(jnp.exp(-jnp.abs(dt_in)))
    da = dtv * (-jnp.exp(alog_ref[...]))
    tri = tri_ref[...]
    a_cs = sum(jnp.dot(tri, part, preferred_element_type=F32)
               for part in _split3(da))
    a_last = a_cs[CHUNK - 1:CHUNK, :]
    w_end = dtv * jnp.exp(a_last - a_cs)
    a_cs_t = a_cs.T
    dt_t = dtv.T
    w_end_t = w_end.T
    expand = expand_ref[...]
    chunk_decay = sum(jnp.dot(part, expand, preferred_element_type=F32)
                      for part in _split3(jnp.broadcast_to(jnp.exp(a_last), (8, LANES)))
                      )[0:1, :]

    row_i = lax.broadcasted_iota(jnp.int32, (CHUNK, CHUNK), 0)
    col_j = lax.broadcasted_iota(jnp.int32, (CHUNK, CHUNK), 1)
    causal = row_i >= col_j
    lane_head = lax.broadcasted_iota(jnp.int32, (CHUNK, GROUP_WIDTH), 1) // SSM_HEAD_DIM
    own_lanes = [jnp.where(lane_head == r, 1.0, 0.0).astype(BF16) for r in range(HEADS_PER_GROUP)]

    for g in range(SSM_GROUPS):
        gcols = slice(g * GROUP_WIDTH, (g + 1) * GROUP_WIDTH)
        xs = conv_silu(xpad_ref, cwx_ref, cbx_ref, g * GROUP_WIDTH, GROUP_WIDTH)
        bm = conv_silu(bcpad_ref, cwbc_ref, cbbc_ref, g * D_STATE, D_STATE)
        cm = conv_silu(bcpad_ref, cwbc_ref, cbbc_ref, D_BC // 2 + g * D_STATE, D_STATE)
        cb = lax.dot_general(cm.astype(BF16), bm.astype(BF16), (((1,), (1,)), ((), ())),
                             preferred_element_type=F32)
        bm_t = bm.T
        prev = state_ref[g]
        xs16 = xs.astype(BF16)
        prev16 = prev.astype(BF16)

        lhs_y, rhs_y, lhs_s, rhs_s = [], [], [], []
        for r in range(HEADS_PER_GROUP):
            h = g * HEADS_PER_GROUP + r
            col_a = jnp.broadcast_to(a_cs[:, h:h + 1], (CHUNK, CHUNK))
            row_a = a_cs_t[h:h + 1, :]
            decay = jnp.exp(jnp.where(causal, col_a - row_a, -jnp.inf))
            lhs_y.append((cb * decay * dt_t[h:h + 1, :]).astype(BF16))
            lhs_y.append((cm * jnp.exp(col_a)).astype(BF16))
            x_own = xs16 * own_lanes[r]
            rhs_y.append(x_own)
            rhs_y.append(prev16 * own_lanes[r])
            lhs_s.append((bm_t * w_end_t[h:h + 1, :]).astype(BF16))
            rhs_s.append(x_own)

        y = jnp.dot(jnp.concatenate(lhs_y, axis=1), jnp.concatenate(rhs_y, axis=0),
                    preferred_element_type=F32)
        s_new = jnp.dot(jnp.concatenate(lhs_s, axis=1), jnp.concatenate(rhs_s, axis=0),
                        preferred_element_type=F32)
        state_ref[g] = prev * chunk_decay[:, gcols] + s_new

        y = y + dsk_ref[:, gcols] * xs
        gated = y * _silu(z_ref[:, gcols])
        y_ref[:, gcols] = _rms_scale(gated, nw_ref[:, gcols]).astype(BF16)

    xpad_ref[0:CONV_PAD, :] = x_ref[CHUNK - CONV_PAD:CHUNK, :]
    bcpad_ref[0:CONV_PAD, :] = bc_ref[CHUNK - CONV_PAD:CHUNK, :]


def _ssd(zxbc, dt_raw, conv_w, conv_b, dt_bias, a_log, d_skip, ssm_norm_w, batch, seq):
    tokens = zxbc.shape[0]
    n_chunks = seq // CHUNK
    pad_heads = LANES - SSM_HEADS
    dtb = jnp.pad(dt_bias.astype(F32), (0, pad_heads)).reshape(1, LANES)
    alog = jnp.pad(a_log.astype(F32), (0, pad_heads)).reshape(1, LANES)
    dsk = jnp.repeat(d_skip.astype(F32), SSM_HEAD_DIM).reshape(1, D_SSM)
    tri = jnp.asarray(np.tril(np.ones((CHUNK, CHUNK), np.float32)), dtype=BF16)
    expand = np.zeros((LANES, D_SSM), np.float32)
    for h in range(SSM_HEADS):
        expand[h, h * SSM_HEAD_DIM:(h + 1) * SSM_HEAD_DIM] = 1.0
    expand = jnp.asarray(expand, dtype=BF16)

    def rows(b, c):
        return b * n_chunks + c

    def full(shape):
        return pl.BlockSpec(shape, lambda b, c: (0,) * len(shape))

    return pl.pallas_call(
        _ssd_kernel,
        grid=(batch, n_chunks),
        in_specs=[
            pl.BlockSpec((CHUNK, D_SSM), lambda b, c: (rows(b, c), 0)),
            pl.BlockSpec((CHUNK, D_SSM), lambda b, c: (rows(b, c), 1)),
            pl.BlockSpec((CHUNK, D_BC), lambda b, c: (rows(b, c), 2)),
            pl.BlockSpec((CHUNK, LANES), lambda b, c: (rows(b, c), 0)),
            full((CONV_WIDTH, D_SSM)), full((1, D_SSM)),
            full((CONV_WIDTH, D_BC)), full((1, D_BC)),
            full((1, LANES)), full((1, LANES)), full((1, D_SSM)), full((1, D_SSM)),
            full((CHUNK, CHUNK)), full((LANES, D_SSM)),
        ],
        out_specs=pl.BlockSpec((CHUNK, D_SSM), lambda b, c: (rows(b, c), 0)),
        out_shape=jax.ShapeDtypeStruct((tokens, D_SSM), BF16),
        scratch_shapes=[
            pltpu.VMEM((SSM_GROUPS, D_STATE, GROUP_WIDTH), F32),
            pltpu.VMEM((CONV_PAD + CHUNK, D_SSM), F32),
            pltpu.VMEM((CONV_PAD + CHUNK, D_BC), F32),
        ],
        compiler_params=_params(("arbitrary", "arbitrary")),
        name="ssd",
    )(zxbc, zxbc, zxbc, dt_raw,
      conv_w[:, :D_SSM], conv_b[:D_SSM].reshape(1, D_SSM),
      conv_w[:, D_SSM:], conv_b[D_SSM:].reshape(1, D_BC),
      dtb, alog, dsk, ssm_norm_w.reshape(1, D_SSM), tri, expand)


BRANCHES = ((16, 16, 16), (4, 32, 32), (1, 128, 128))
ATT_GROUP = (4, 8, 8)


def _branch_bias(n_runs, width, back):
    s_q = np.repeat(np.arange(n_runs), width)
    i_q = np.tile(np.arange(width), n_runs)
    j_q = n_runs * i_q + s_q
    s_p = np.repeat(np.arange(n_runs), back)
    i_p = np.tile(np.arange(back), n_runs) - back
    j_k = np.concatenate([n_runs * i_p + s_p, j_q])
    dist = j_q[:, None] - j_k[None, :]
    return np.where((dist >= 0) & (dist <= WINDOW_REACH), 0.0, MASKED).astype(np.float32)


def _attn_kernel(q_ref, k_ref, v_ref, b0_ref, b1_ref, b2_ref, o_ref,
                 num_ref, m_ref, l_ref, nat_ref, s0_ref, p0_ref, s1_ref, p1_ref, *, run_len):
    bias_refs = (b0_ref, b1_ref, b2_ref)
    stage_refs = ((s0_ref, p0_ref), (s1_ref, p1_ref), (s1_ref, p1_ref))

    def rows_of(ref, starts, size):
        parts = [ref[st:st + size, :] for st in starts]
        return parts[0] if len(parts) == 1 else jnp.concatenate(parts, axis=0)

    def keys_of(ref, branch, starts, with_prev):
        n_runs, width, back = BRANCHES[branch]
        cur = rows_of(ref, starts, width)
        if not with_prev:
            return cur
        return jnp.concatenate([rows_of(ref, [st - back for st in starts], back), cur], axis=0)

    def layout(branch, blocks, slot):
        n_runs, width, back = BRANCHES[branch]
        n_cls = MAX_DILATION // n_runs
        out = []
        for u, (cls, i0, with_prev) in enumerate(blocks):
            starts = [(cls + n_cls * s) * run_len + i0 for s in range(n_runs)]
            out.append((slot * ATT_GROUP[branch] + u, starts, with_prev,
                        n_runs * (width + (back if with_prev else 0))))
        return out

    def issue(branch, blocks, slot):
        n_runs, width, back = BRANCHES[branch]
        s_ref, _ = stage_refs[branch]
        for idx, starts, with_prev, n_keys in layout(branch, blocks, slot):
            q = rows_of(q_ref, starts, width)
            k = keys_of(k_ref, branch, starts, with_prev)
            bias = bias_refs[branch][...] if with_prev else bias_refs[branch][:, n_runs * back:]
            s_ref[idx, :, 0:n_keys] = lax.dot_general(
                q, k, (((1,), (1,)), ((), ())), preferred_element_type=F32) + bias

    def finish(branch, blocks, slot):
        n_runs, width, back = BRANCHES[branch]
        s_ref, p_ref = stage_refs[branch]
        plan = layout(branch, blocks, slot)
        for idx, starts, with_prev, n_keys in plan:
            s = s_ref[idx, :, 0:n_keys]
            m = jnp.max(s, axis=-1, keepdims=True)
            p_ref[idx, :, 0:n_keys] = jnp.exp(s - m).astype(BF16)
            for part, st in enumerate(starts):
                m_ref[branch, st:st + width, :] = jnp.broadcast_to(
                    m[part * width:(part + 1) * width], (width, ATT_HEAD_DIM))
        for idx, starts, with_prev, n_keys in plan:
            v = keys_of(v_ref, branch, starts, with_prev)
            v1 = jnp.concatenate([v, jnp.ones((n_keys, ATT_HEAD_DIM), BF16)], axis=1)
            pv = jnp.dot(p_ref[idx, :, 0:n_keys], v1, preferred_element_type=F32)
            for part, st in enumerate(starts):
                rows = slice(part * width, (part + 1) * width)
                num_ref[branch, st:st + width, :] = pv[rows, 0:ATT_HEAD_DIM]
                l_ref[branch, st:st + width, :] = pv[rows, ATT_HEAD_DIM:]

    groups = []
    for branch, (n_runs, width, back) in enumerate(BRANCHES):
        blocks = [(cls, i0, i0 > 0) for cls in range(MAX_DILATION // n_runs) for i0 in range(0, run_len, width)]
        size = ATT_GROUP[branch]
        groups += [(branch, blocks[g:g + size]) for g in range(0, len(blocks), size)]
    issue(*groups[0], 0)
    for n, group in enumerate(groups):
        if n + 1 < len(groups):
            issue(*groups[n + 1], (n + 1) % 2)
        finish(*group, n % 2)

    def combine(r, carry):
        sl = pl.ds(pl.multiple_of(r * run_len, run_len), run_len)
        m0, m1, m2 = m_ref[0, sl, :], m_ref[1, sl, :], m_ref[2, sl, :]
        top = jnp.maximum(jnp.maximum(m0, m1), m2)
        w0, w1, w2 = jnp.exp(m0 - top), jnp.exp(m1 - top), jnp.exp(m2 - top)
        den = w0 * l_ref[0, sl, :] + w1 * l_ref[1, sl, :] + w2 * l_ref[2, sl, :]
        num = w0 * num_ref[0, sl, :] + w1 * num_ref[1, sl, :] + w2 * num_ref[2, sl, :]
        nat_ref[pl.ds(r, run_len, stride=MAX_DILATION), :] = num / den
        return carry
    lax.fori_loop(0, MAX_DILATION, combine, 0)

    rows = 256

    def emit(c, carry):
        sl = pl.ds(pl.multiple_of(c * rows, rows), rows)
        o_ref[sl, :] = nat_ref[sl, :].astype(BF16)
        return carry
    lax.fori_loop(0, (MAX_DILATION * run_len) // rows, emit, 0)


def _stage_scratch():
    out = []
    for branch in (0, 1):
        n_runs, width, back = BRANCHES[branch]
        shape = (2 * ATT_GROUP[branch], n_runs * width, n_runs * (width + back))
        out += [pltpu.VMEM(shape, F32), pltpu.VMEM(shape, BF16)]
    return out


def _attention(qkv, batch, seq):
    run_len = seq // MAX_DILATION
    qkv2 = qkv.reshape(batch, seq, 3 * D_ATT)
    biases = [jnp.asarray(_branch_bias(*b)) for b in BRANCHES]

    def head_block(offset):
        return pl.BlockSpec((None, seq, ATT_HEAD_DIM), lambda b, h: (b, 0, offset + h))

    def full(shape):
        return pl.BlockSpec(shape, lambda b, h: (0,) * len(shape))

    return pl.pallas_call(
        functools.partial(_attn_kernel, run_len=run_len),
        grid=(batch, ATT_HEADS),
        in_specs=[head_block(0), head_block(ATT_HEADS), head_block(2 * ATT_HEADS)]
        + [full(b.shape) for b in biases],
        out_specs=pl.BlockSpec((None, seq, ATT_HEAD_DIM), lambda b, h: (b, 0, h)),
        out_shape=jax.ShapeDtypeStruct((batch, seq, D_ATT), BF16),
        scratch_shapes=[
            pltpu.VMEM((len(BRANCHES), seq, ATT_HEAD_DIM), F32),
            pltpu.VMEM((len(BRANCHES), seq, ATT_HEAD_DIM), F32),
            pltpu.VMEM((len(BRANCHES), seq, ATT_HEAD_DIM), F32),
            pltpu.VMEM((seq, ATT_HEAD_DIM), F32),
        ] + _stage_scratch(),
        compiler_params=_params(("arbitrary", "arbitrary")),
        name="dilated_attn",
    )(qkv2, qkv2, qkv2, *biases)


OUT_TM = 512


def _outproj_kernel(ys_ref, ya_ref, w_ref, x_ref, gpost_ref, gpre_ref, h_ref, u_ref, mix_ref):
    mix_ref[...] = jnp.dot(ys_ref[...], w_ref[0:D_SSM, :], preferred_element_type=F32)
    mix_ref[...] += jnp.dot(ya_ref[...], w_ref[D_SSM:D_MIX, :], preferred_element_type=F32)
    chunk = 128

    def body(c, carry):
        sl = pl.ds(pl.multiple_of(c * chunk, chunk), chunk)
        h = x_ref[sl, :] + _rms_scale(mix_ref[sl, :], gpost_ref[...])
        h_ref[sl, :] = h
        u_ref[sl, :] = _rms_scale(h, gpre_ref[...]).astype(BF16)
        return carry
    lax.fori_loop(0, OUT_TM // chunk, body, 0)


def _out_projection(y_ssm, y_att, w_out, x2, g_post, g_pre):
    tokens = x2.shape[0]

    def rows(width):
        return pl.BlockSpec((OUT_TM, width), lambda i: (i, 0))

    def whole(shape):
        return pl.BlockSpec(shape, lambda i: (0, 0), pipeline_mode=pl.Buffered(1))

    return pl.pallas_call(
        _outproj_kernel,
        grid=(tokens // OUT_TM,),
        in_specs=[rows(D_SSM), rows(D_ATT), whole((D_MIX, D_MODEL)), rows(D_MODEL),
                  whole((1, D_MODEL)), whole((1, D_MODEL))],
        out_specs=[rows(D_MODEL), rows(D_MODEL)],
        out_shape=[
            jax.ShapeDtypeStruct((tokens, D_MODEL), F32),
            jax.ShapeDtypeStruct((tokens, D_MODEL), BF16),
        ],
        scratch_shapes=[pltpu.VMEM((OUT_TM, D_MODEL), F32)],
        compiler_params=_params(("arbitrary",)),
        name="outproj",
    )(y_ssm, y_att, w_out, x2, g_post, g_pre)


MLP_TM = 512
MLP_TF = 1024


def _mlp_kernel(u_ref, wup_ref, wdown_ref, h_ref, g_ref, o_ref, acc_ref):
    f = pl.program_id(1)

    @pl.when(f == 0)
    def _():
        acc_ref[...] = jnp.zeros_like(acc_ref)

    hid = jnp.maximum(jnp.dot(u_ref[...], wup_ref[...], preferred_element_type=F32), 0.0)
    acc_ref[...] += jnp.dot((hid * hid).astype(BF16), wdown_ref[...], preferred_element_type=F32)

    @pl.when(f == pl.num_programs(1) - 1)
    def _():
        chunk = 128

        def body(c, carry):
            sl = pl.ds(pl.multiple_of(c * chunk, chunk), chunk)
            o_ref[sl, :] = h_ref[sl, :] + _rms_scale(acc_ref[sl, :], g_ref[...])
            return carry
        lax.fori_loop(0, MLP_TM // chunk, body, 0)


def _mlp(u2, w_up, w_down, h1, g_post):
    tokens = u2.shape[0]
    return pl.pallas_call(
        _mlp_kernel,
        grid=(tokens // MLP_TM, D_FF // MLP_TF),
        in_specs=[
            pl.BlockSpec((MLP_TM, D_MODEL), lambda i, f: (i, 0)),
            pl.BlockSpec((D_MODEL, MLP_TF), lambda i, f: (0, f)),
            pl.BlockSpec((MLP_TF, D_MODEL), lambda i, f: (f, 0)),
            pl.BlockSpec((MLP_TM, D_MODEL), lambda i, f: (i, 0)),
            pl.BlockSpec((1, D_MODEL), lambda i, f: (0, 0)),
        ],
        out_specs=pl.BlockSpec((MLP_TM, D_MODEL), lambda i, f: (i, 0)),
        out_shape=jax.ShapeDtypeStruct((tokens, D_MODEL), F32),
        scratch_shapes=[pltpu.VMEM((MLP_TM, D_MODEL), F32)],
        compiler_params=_params(("arbitrary", "arbitrary")),
        name="mlp",
    )(u2, w_up, w_down, h1, g_post)


def kernel(x, norm_mix_pre, w_in, conv_w, conv_b, dt_bias, a_log, d_skip, ssm_norm_w, w_out,
           norm_mix_post, norm_mlp_pre, w_up, w_down, norm_mlp_post):
    batch, seq, _ = x.shape
    depth = w_in.shape[0]
    h = x.reshape(batch * seq, D_MODEL)
    for layer in range(depth):
        w_main, w_dt = _prep_in_weights(w_in[layer])
        zxbc, dt_raw, qkv = _in_projection(
            h, norm_mix_pre[layer].reshape(1, D_MODEL), w_main, w_dt, batch, seq)
        y_ssm = _ssd(zxbc, dt_raw, conv_w[layer], conv_b[layer], dt_bias[layer], a_log[layer],
                     d_skip[layer], ssm_norm_w[layer], batch, seq)
        y_att = _attention(qkv, batch, seq).reshape(batch * seq, D_ATT)
        h1, u2 = _out_projection(
            y_ssm, y_att, w_out[layer].astype(BF16), h,
            norm_mix_post[layer].reshape(1, D_MODEL), norm_mlp_pre[layer].reshape(1, D_MODEL))
        h = _mlp(u2, w_up[layer].astype(BF16), w_down[layer].astype(BF16), h1,
                 norm_mlp_post[layer].reshape(1, D_MODEL))
    return h.reshape(batch, seq, D_MODEL)
```

```python
import functools

import numpy as np
import jax
import jax.numpy as jnp
from jax import lax
from jax.experimental import pallas as pl
from jax.experimental.pallas import tpu as pltpu

F32 = jnp.float32
BF16 = jnp.bfloat16

D_MODEL = 2048
SSM_HEAD_DIM = 64
SSM_HEADS = 32
SSM_GROUPS = 8
HEADS_PER_GROUP = 4
D_STATE = 128
D_SSM = SSM_HEADS * SSM_HEAD_DIM
D_BC = 2 * SSM_GROUPS * D_STATE
GROUP_WIDTH = HEADS_PER_GROUP * SSM_HEAD_DIM
CONV_WIDTH = 4
CHUNK = 128
ATT_HEADS = 16
ATT_HEAD_DIM = 128
D_ATT = ATT_HEADS * ATT_HEAD_DIM
D_MIX = D_SSM + D_ATT
D_FF = 4 * D_MODEL
WINDOW_REACH = 128
MAX_DILATION = 16
EPS = 1e-6
ATT_SCALE = ATT_HEAD_DIM ** -0.5
MASKED = -1e30

LANES = 128
VMEM_LIMIT = 56 * 1024 * 1024


def _params(semantics):
    return pltpu.CompilerParams(dimension_semantics=semantics, vmem_limit_bytes=VMEM_LIMIT)


def _silu(v):
    return v * (1.0 / (1.0 + jnp.exp(-v)))


def _split3(v):
    hi = v.astype(BF16)
    rest = v - hi.astype(F32)
    mid = rest.astype(BF16)
    lo = (rest - mid.astype(F32)).astype(BF16)
    return hi, mid, lo


def _rms_scale(v, gain):
    ms = jnp.mean(v * v, axis=-1, keepdims=True)
    return (v * lax.rsqrt(ms + EPS)) * gain


PREP_COLS = 512


def _prep_kernel(wt_ref, wdt_ref, main_ref, dt_ref):
    main_ref[...] = wt_ref[...].T.astype(BF16)

    @pl.when(pl.program_id(0) == 0)
    def _():
        padded = jnp.concatenate([wdt_ref[...], jnp.zeros((LANES - SSM_HEADS, D_MODEL), F32)], axis=0)
        dt_ref[...] = padded.T.astype(BF16)


def _prep_in_weights(w):
    wt = w.T
    ssm_cols = 2 * D_SSM + D_BC
    att0 = ssm_cols + SSM_HEADS
    main_cols = ssm_cols + 3 * D_ATT
    n_ssm = ssm_cols // PREP_COLS

    def feature_row(j):
        return pl.multiple_of(jnp.where(j < n_ssm, j * PREP_COLS, att0 + (j - n_ssm) * PREP_COLS), 8)

    return pl.pallas_call(
        _prep_kernel,
        grid=(main_cols // PREP_COLS,),
        in_specs=[
            pl.BlockSpec((pl.Element(PREP_COLS), pl.Element(D_MODEL)), lambda j: (feature_row(j), 0)),
            pl.BlockSpec((pl.Element(SSM_HEADS), pl.Element(D_MODEL)), lambda j: (ssm_cols, 0)),
        ],
        out_specs=[pl.BlockSpec((D_MODEL, PREP_COLS), lambda j: (0, j)),
                   pl.BlockSpec((D_MODEL, LANES), lambda j: (0, 0))],
        out_shape=[jax.ShapeDtypeStruct((D_MODEL, main_cols), BF16),
                   jax.ShapeDtypeStruct((D_MODEL, LANES), BF16)],
        compiler_params=_params(("arbitrary",)),
        name="inproj_weights",
    )(wt, wt)


IN_TM = 1024
IN_TN = 1024
IN_STAGE_SLABS = 4
IN_STAGE_PITCH = 24


def _inproj_kernel(x_ref, g_ref, w_ref, wdt_ref, zxbc_ref, dt_ref, qkv_ref, u_ref, uperm_ref, uf_ref,
                   *, n_ssm_tiles):
    j = pl.program_id(1)
    chunk = 128
    n_slabs = D_MODEL // LANES
    rows = IN_TM // MAX_DILATION

    @pl.when(j == 0)
    def _():
        def body(c, carry):
            sl = pl.ds(pl.multiple_of(c * chunk, chunk), chunk)
            u_ref[sl, :] = _rms_scale(x_ref[sl, :], g_ref[...]).astype(BF16)
            return carry
        lax.fori_loop(0, IN_TM // chunk, body, 0)
        dt_ref[...] = jnp.dot(u_ref[...], wdt_ref[...], preferred_element_type=F32)
        for s0 in range(0, n_slabs, IN_STAGE_SLABS):
            for s in range(IN_STAGE_SLABS):
                cols = slice((s0 + s) * LANES, (s0 + s + 1) * LANES)
                for grp in range(rows):
                    uf_ref[s, grp * IN_STAGE_PITCH:grp * IN_STAGE_PITCH + MAX_DILATION, :] = (
                        u_ref[grp * MAX_DILATION:(grp + 1) * MAX_DILATION, cols].astype(F32))
            for r in range(MAX_DILATION):
                for s in range(IN_STAGE_SLABS):
                    uperm_ref[r * rows:(r + 1) * rows, (s0 + s) * LANES:(s0 + s + 1) * LANES] = (
                        uf_ref[s, pl.ds(r, rows, stride=IN_STAGE_PITCH), :].astype(BF16))

    @pl.when(j < n_ssm_tiles)
    def _():
        zxbc_ref[...] = jnp.dot(u_ref[...], w_ref[...], preferred_element_type=F32)

    @pl.when(j >= n_ssm_tiles)
    def _():
        scale = jnp.where(j < n_ssm_tiles + D_ATT // IN_TN, ATT_SCALE, 1.0).astype(F32)
        res = jnp.dot(uperm_ref[...], w_ref[...], preferred_element_type=F32) * scale
        for r in range(MAX_DILATION):
            qkv_ref[r] = res[r * rows:(r + 1) * rows, :].astype(BF16)


def _in_projection(x2, gain, w_main, w_dt, batch, seq):
    tokens = x2.shape[0]
    n_m = tokens // IN_TM
    m_per_seq = seq // IN_TM
    ssm_cols = 2 * D_SSM + D_BC
    att_cols = 3 * D_ATT
    n_ssm_tiles = ssm_cols // IN_TN
    n_att_tiles = att_cols // IN_TN
    run_len = seq // MAX_DILATION
    return pl.pallas_call(
        functools.partial(_inproj_kernel, n_ssm_tiles=n_ssm_tiles),
        grid=(n_m, n_ssm_tiles + n_att_tiles),
        in_specs=[
            pl.BlockSpec((IN_TM, D_MODEL), lambda i, j: (i, 0)),
            pl.BlockSpec((1, D_MODEL), lambda i, j: (0, 0)),
            pl.BlockSpec((D_MODEL, IN_TN), lambda i, j: (0, j)),
            pl.BlockSpec((D_MODEL, LANES), lambda i, j: (0, 0)),
        ],
        out_specs=[
            pl.BlockSpec((IN_TM, IN_TN), lambda i, j: (i, jnp.minimum(j, n_ssm_tiles - 1))),
            pl.BlockSpec((IN_TM, LANES), lambda i, j: (i, 0)),
            pl.BlockSpec((None, MAX_DILATION, IN_TM // MAX_DILATION, IN_TN),
                         lambda i, j: (i // m_per_seq, 0, i % m_per_seq, jnp.maximum(j - n_ssm_tiles, 0))),
        ],
        out_shape=[
            jax.ShapeDtypeStruct((tokens, ssm_cols), F32),
            jax.ShapeDtypeStruct((tokens, LANES), F32),
            jax.ShapeDtypeStruct((batch, MAX_DILATION, run_len, att_cols), BF16),
        ],
        scratch_shapes=[
            pltpu.VMEM((IN_TM, D_MODEL), BF16),
            pltpu.VMEM((IN_TM, D_MODEL), BF16),
            pltpu.VMEM((IN_STAGE_SLABS, IN_TM // MAX_DILATION * IN_STAGE_PITCH, LANES), F32),
        ],
        compiler_params=_params(("arbitrary", "arbitrary")),
        name="inproj",
    )(x2, gain, w_main, w_dt)


CONV_PAD = 8
SSD_ROWS = 2 * CHUNK


def _ssd_kernel(z_ref, x_ref, bc_ref, dt_ref, cwx_ref, cbx_ref, cwbc_ref, cbbc_ref,
                dtb_ref, alog_ref, dsk_ref, nw_ref, tri_ref, expand_ref,
                y_ref, state_ref, xpad_ref, bcpad_ref):
    @pl.when(pl.program_id(1) == 0)
    def _():
        state_ref[...] = jnp.zeros_like(state_ref)
        xpad_ref[0:CONV_PAD, :] = jnp.zeros((CONV_PAD, D_SSM), F32)
        bcpad_ref[0:CONV_PAD, :] = jnp.zeros((CONV_PAD, D_BC), F32)

    xpad_ref[CONV_PAD:CONV_PAD + SSD_ROWS, :] = x_ref[...]
    bcpad_ref[CONV_PAD:CONV_PAD + SSD_ROWS, :] = bc_ref[...]

    def conv_silu(pad_ref, w_ref, b_ref, r0, c0, width):
        cols = slice(c0, c0 + width)
        acc = b_ref[:, cols]
        for k in range(CONV_WIDTH):
            start = r0 + CONV_PAD - (CONV_WIDTH - 1) + k
            acc = acc + w_ref[k:k + 1, cols] * pad_ref[start:start + CHUNK, cols]
        return _silu(acc)

    tri = tri_ref[...]
    expand = expand_ref[...]
    row_i = lax.broadcasted_iota(jnp.int32, (CHUNK, CHUNK), 0)
    col_j = lax.broadcasted_iota(jnp.int32, (CHUNK, CHUNK), 1)
    causal = row_i >= col_j
    lane_head = lax.broadcasted_iota(jnp.int32, (CHUNK, GROUP_WIDTH), 1) // SSM_HEAD_DIM
    own_lanes = [jnp.where(lane_head == r, 1.0, 0.0).astype(BF16) for r in range(HEADS_PER_GROUP)]

    for r0 in range(0, SSD_ROWS, CHUNK):
        rows = slice(r0, r0 + CHUNK)
        dt_in = dt_ref[rows, :] + dtb_ref[...]
        dtv = jnp.maximum(dt_in, 0.0) + jnp.log1p(jnp.exp(-jnp.abs(dt_in)))
        da = dtv * (-jnp.exp(alog_ref[...]))
        a_cs = sum(jnp.dot(tri, part, preferred_element_type=F32)
                   for part in _split3(da))
        a_last = a_cs[CHUNK - 1:CHUNK, :]
        w_end = dtv * jnp.exp(a_last - a_cs)
        a_cs_t = a_cs.T
        dt_t = dtv.T
        w_end_t = w_end.T
        chunk_decay = sum(jnp.dot(part, expand, preferred_element_type=F32)
                          for part in _split3(jnp.broadcast_to(jnp.exp(a_last), (8, LANES)))
                          )[0:1, :]

        for g in range(SSM_GROUPS):
            gcols = slice(g * GROUP_WIDTH, (g + 1) * GROUP_WIDTH)
            xs = conv_silu(xpad_ref, cwx_ref, cbx_ref, r0, g * GROUP_WIDTH, GROUP_WIDTH)
            bm = conv_silu(bcpad_ref, cwbc_ref, cbbc_ref, r0, g * D_STATE, D_STATE)
            cm = conv_silu(bcpad_ref, cwbc_ref, cbbc_ref, r0, D_BC // 2 + g * D_STATE, D_STATE)
            cb = lax.dot_general(cm.astype(BF16), bm.astype(BF16), (((1,), (1,)), ((), ())),
                                 preferred_element_type=F32)
            bm_t = bm.T
            prev = state_ref[g]
            xs16 = xs.astype(BF16)
            prev16 = prev.astype(BF16)

            lhs_y, rhs_y, lhs_s, rhs_s = [], [], [], []
            for r in range(HEADS_PER_GROUP):
                h = g * HEADS_PER_GROUP + r
                col_a = jnp.broadcast_to(a_cs[:, h:h + 1], (CHUNK, CHUNK))
                row_a = a_cs_t[h:h + 1, :]
                decay = jnp.exp(jnp.where(causal, col_a - row_a, -jnp.inf))
                lhs_y.append((cb * decay * dt_t[h:h + 1, :]).astype(BF16))
                lhs_y.append((cm * jnp.exp(col_a)).astype(BF16))
                x_own = xs16 * own_lanes[r]
                rhs_y.append(x_own)
                rhs_y.append(prev16 * own_lanes[r])
                lhs_s.append((bm_t * w_end_t[h:h + 1, :]).astype(BF16))
                rhs_s.append(x_own)

            y = jnp.dot(jnp.concatenate(lhs_y, axis=1), jnp.concatenate(rhs_y, axis=0),
                        preferred_element_type=F32)
            s_new = jnp.dot(jnp.concatenate(lhs_s, axis=1), jnp.concatenate(rhs_s, axis=0),
                            preferred_element_type=F32)
            state_ref[g] = prev * chunk_decay[:, gcols] + s_new

            y = y + dsk_ref[:, gcols] * xs
            gated = y * _silu(z_ref[rows, gcols])
            y_ref[rows, gcols] = _rms_scale(gated, nw_ref[:, gcols]).astype(BF16)

    xpad_ref[0:CONV_PAD, :] = x_ref[SSD_ROWS - CONV_PAD:SSD_ROWS, :]
    bcpad_ref[0:CONV_PAD, :] = bc_ref[SSD_ROWS - CONV_PAD:SSD_ROWS, :]


def _ssd(zxbc, dt_raw, conv_w, conv_b, dt_bias, a_log, d_skip, ssm_norm_w, batch, seq):
    tokens = zxbc.shape[0]
    n_steps = seq // SSD_ROWS
    pad_heads = LANES - SSM_HEADS
    dtb = jnp.pad(dt_bias.astype(F32), (0, pad_heads)).reshape(1, LANES)
    alog = jnp.pad(a_log.astype(F32), (0, pad_heads)).reshape(1, LANES)
    dsk = jnp.repeat(d_skip.astype(F32), SSM_HEAD_DIM).reshape(1, D_SSM)
    tri = jnp.asarray(np.tril(np.ones((CHUNK, CHUNK), np.float32)), dtype=BF16)
    expand = np.zeros((LANES, D_SSM), np.float32)
    for h in range(SSM_HEADS):
        expand[h, h * SSM_HEAD_DIM:(h + 1) * SSM_HEAD_DIM] = 1.0
    expand = jnp.asarray(expand, dtype=BF16)

    def rows(b, c):
        return b * n_steps + c

    def full(shape):
        return pl.BlockSpec(shape, lambda b, c: (0,) * len(shape))

    return pl.pallas_call(
        _ssd_kernel,
        grid=(batch, n_steps),
        in_specs=[
            pl.BlockSpec((SSD_ROWS, D_SSM), lambda b, c: (rows(b, c), 0)),
            pl.BlockSpec((SSD_ROWS, D_SSM), lambda b, c: (rows(b, c), 1)),
            pl.BlockSpec((SSD_ROWS, D_BC), lambda b, c: (rows(b, c), 2)),
            pl.BlockSpec((SSD_ROWS, LANES), lambda b, c: (rows(b, c), 0)),
            full((CONV_WIDTH, D_SSM)), full((1, D_SSM)),
            full((CONV_WIDTH, D_BC)), full((1, D_BC)),
            full((1, LANES)), full((1, LANES)), full((1, D_SSM)), full((1, D_SSM)),
            full((CHUNK, CHUNK)), full((LANES, D_SSM)),
        ],
        out_specs=pl.BlockSpec((SSD_ROWS, D_SSM), lambda b, c: (rows(b, c), 0)),
        out_shape=jax.ShapeDtypeStruct((tokens, D_SSM), BF16),
        scratch_shapes=[
            pltpu.VMEM((SSM_GROUPS, D_STATE, GROUP_WIDTH), F32),
            pltpu.VMEM((CONV_PAD + SSD_ROWS, D_SSM), F32),
            pltpu.VMEM((CONV_PAD + SSD_ROWS, D_BC), F32),
        ],
        compiler_params=_params(("arbitrary", "arbitrary")),
        name="ssd",
    )(zxbc, zxbc, zxbc, dt_raw,
      conv_w[:, :D_SSM], conv_b[:D_SSM].reshape(1, D_SSM),
      conv_w[:, D_SSM:], conv_b[D_SSM:].reshape(1, D_BC),
      dtb, alog, dsk, ssm_norm_w.reshape(1, D_SSM), tri, expand)


BRANCHES = ((16, 16, 16), (4, 32, 32), (1, 128, 128))
ATT_GROUP = (4, 8, 8)


def _branch_bias(n_runs, width, back):
    s_q = np.repeat(np.arange(n_runs), width)
    i_q = np.tile(np.arange(width), n_runs)
    j_q = n_runs * i_q + s_q
    s_p = np.repeat(np.arange(n_runs), back)
    i_p = np.tile(np.arange(back), n_runs) - back
    j_k = np.concatenate([n_runs * i_p + s_p, j_q])
    dist = j_q[:, None] - j_k[None, :]
    return np.where((dist >= 0) & (dist <= WINDOW_REACH), 0.0, MASKED).astype(np.float32)


def _attn_kernel(q_ref, k_ref, v_ref, b0_ref, b1_ref, b2_ref, o_ref,
                 num_ref, m_ref, l_ref, nat_ref, s0_ref, p0_ref, s1_ref, p1_ref, *, run_len):
    bias_refs = (b0_ref, b1_ref, b2_ref)
    stage_refs = ((s0_ref, p0_ref), (s1_ref, p1_ref), (s1_ref, p1_ref))

    def rows_of(ref, starts, size):
        parts = [ref[st:st + size, :] for st in starts]
        return parts[0] if len(parts) == 1 else jnp.concatenate(parts, axis=0)

    def keys_of(ref, branch, starts, with_prev):
        n_runs, width, back = BRANCHES[branch]
        cur = rows_of(ref, starts, width)
        if not with_prev:
            return cur
        return jnp.concatenate([rows_of(ref, [st - back for st in starts], back), cur], axis=0)

    def layout(branch, blocks, slot):
        n_runs, width, back = BRANCHES[branch]
        n_cls = MAX_DILATION // n_runs
        out = []
        for u, (cls, i0, with_prev) in enumerate(blocks):
            starts = [(cls + n_cls * s) * run_len + i0 for s in range(n_runs)]
            out.append((slot * ATT_GROUP[branch] + u, starts, with_prev,
                        n_runs * (width + (back if with_prev else 0))))
        return out

    def issue(branch, blocks, slot):
        n_runs, width, back = BRANCHES[branch]
        s_ref, _ = stage_refs[branch]
        for idx, starts, with_prev, n_keys in layout(branch, blocks, slot):
            q = rows_of(q_ref, starts, width)
            k = keys_of(k_ref, branch, starts, with_prev)
            bias = bias_refs[branch][...] if with_prev else bias_refs[branch][:, n_runs * back:]
            s_ref[idx, :, 0:n_keys] = lax.dot_general(
                q, k, (((1,), (1,)), ((), ())), preferred_element_type=F32) + bias

    def finish(branch, blocks, slot):
        n_runs, width, back = BRANCHES[branch]
        s_ref, p_ref = stage_refs[branch]
        plan = layout(branch, blocks, slot)
        for idx, starts, with_prev, n_keys in plan:
            s = s_ref[idx, :, 0:n_keys]
            m = jnp.max(s, axis=-1, keepdims=True)
            p_ref[idx, :, 0:n_keys] = jnp.exp(s - m).astype(BF16)
            for part, st in enumerate(starts):
                m_ref[branch, st:st + width, :] = jnp.broadcast_to(
                    m[part * width:(part + 1) * width], (width, ATT_HEAD_DIM))
        for idx, starts, with_prev, n_keys in plan:
            v = keys_of(v_ref, branch, starts, with_prev)
            v1 = jnp.concatenate([v, jnp.ones((n_keys, ATT_HEAD_DIM), BF16)], axis=1)
            pv = jnp.dot(p_ref[idx, :, 0:n_keys], v1, preferred_element_type=F32)
            for part, st in enumerate(starts):
                rows = slice(part * width, (part + 1) * width)
                num_ref[branch, st:st + width, :] = pv[rows, 0:ATT_HEAD_DIM]
                l_ref[branch, st:st + width, :] = pv[rows, ATT_HEAD_DIM:]

    groups = []
    for branch, (n_runs, width, back) in enumerate(BRANCHES):
        blocks = [(cls, i0, i0 > 0) for cls in range(MAX_DILATION // n_runs) for i0 in range(0, run_len, width)]
        size = ATT_GROUP[branch]
        groups += [(branch, blocks[g:g + size]) for g in range(0, len(blocks), size)]
    issue(*groups[0], 0)
    for n, group in enumerate(groups):
        if n + 1 < len(groups):
            issue(*groups[n + 1], (n + 1) % 2)
        finish(*group, n % 2)

    def combine(r, carry):
        sl = pl.ds(pl.multiple_of(r * run_len, run_len), run_len)
        m0, m1, m2 = m_ref[0, sl, :], m_ref[1, sl, :], m_ref[2, sl, :]
        top = jnp.maximum(jnp.maximum(m0, m1), m2)
        w0, w1, w2 = jnp.exp(m0 - top), jnp.exp(m1 - top), jnp.exp(m2 - top)
        den = w0 * l_ref[0, sl, :] + w1 * l_ref[1, sl, :] + w2 * l_ref[2, sl, :]
        num = w0 * num_ref[0, sl, :] + w1 * num_ref[1, sl, :] + w2 * num_ref[2, sl, :]
        nat_ref[pl.ds(r, run_len, stride=MAX_DILATION), :] = num / den
        return carry
    lax.fori_loop(0, MAX_DILATION, combine, 0)

    rows = 256

    def emit(c, carry):
        sl = pl.ds(pl.multiple_of(c * rows, rows), rows)
        o_ref[sl, :] = nat_ref[sl, :].astype(BF16)
        return carry
    lax.fori_loop(0, (MAX_DILATION * run_len) // rows, emit, 0)


def _stage_scratch():
    out = []
    for branch in (0, 1):
        n_runs, width, back = BRANCHES[branch]
        shape = (2 * ATT_GROUP[branch], n_runs * width, n_runs * (width + back))
        out += [pltpu.VMEM(shape, F32), pltpu.VMEM(shape, BF16)]
    return out


def _attention(qkv, batch, seq):
    run_len = seq // MAX_DILATION
    qkv2 = qkv.reshape(batch, seq, 3 * D_ATT)
    biases = [jnp.asarray(_branch_bias(*b)) for b in BRANCHES]

    def head_block(offset):
        return pl.BlockSpec((None, seq, ATT_HEAD_DIM), lambda b, h: (b, 0, offset + h))

    def full(shape):
        return pl.BlockSpec(shape, lambda b, h: (0,) * len(shape))

    return pl.pallas_call(
        functools.partial(_attn_kernel, run_len=run_len),
        grid=(batch, ATT_HEADS),
        in_specs=[head_block(0), head_block(ATT_HEADS), head_block(2 * ATT_HEADS)]
        + [full(b.shape) for b in biases],
        out_specs=pl.BlockSpec((None, seq, ATT_HEAD_DIM), lambda b, h: (b, 0, h)),
        out_shape=jax.ShapeDtypeStruct((batch, seq, D_ATT), BF16),
        scratch_shapes=[
            pltpu.VMEM((len(BRANCHES), seq, ATT_HEAD_DIM), F32),
            pltpu.VMEM((len(BRANCHES), seq, ATT_HEAD_DIM), F32),
            pltpu.VMEM((len(BRANCHES), seq, ATT_HEAD_DIM), F32),
            pltpu.VMEM((seq, ATT_HEAD_DIM), F32),
        ] + _stage_scratch(),
        compiler_params=_params(("arbitrary", "arbitrary")),
        name="dilated_attn",
    )(qkv2, qkv2, qkv2, *biases)


OUT_TM = 512


def _outproj_kernel(ys_ref, ya_ref, w_ref, x_ref, gpost_ref, gpre_ref, h_ref, u_ref, mix_ref):
    mix_ref[...] = jnp.dot(ys_ref[...], w_ref[0:D_SSM, :], preferred_element_type=F32)
    mix_ref[...] += jnp.dot(ya_ref[...], w_ref[D_SSM:D_MIX, :], preferred_element_type=F32)
    chunk = 128

    def body(c, carry):
        sl = pl.ds(pl.multiple_of(c * chunk, chunk), chunk)
        h = x_ref[sl, :] + _rms_scale(mix_ref[sl, :], gpost_ref[...])
        h_ref[sl, :] = h
        u_ref[sl, :] = _rms_scale(h, gpre_ref[...]).astype(BF16)
        return carry
    lax.fori_loop(0, OUT_TM // chunk, body, 0)


def _out_projection(y_ssm, y_att, w_out, x2, g_post, g_pre):
    tokens = x2.shape[0]

    def rows(width):
        return pl.BlockSpec((OUT_TM, width), lambda i: (i, 0))

    def whole(shape):
        return pl.BlockSpec(shape, lambda i: (0, 0), pipeline_mode=pl.Buffered(1))

    return pl.pallas_call(
        _outproj_kernel,
        grid=(tokens // OUT_TM,),
        in_specs=[rows(D_SSM), rows(D_ATT), whole((D_MIX, D_MODEL)), rows(D_MODEL),
                  whole((1, D_MODEL)), whole((1, D_MODEL))],
        out_specs=[rows(D_MODEL), rows(D_MODEL)],
        out_shape=[
            jax.ShapeDtypeStruct((tokens, D_MODEL), F32),
            jax.ShapeDtypeStruct((tokens, D_MODEL), BF16),
        ],
        scratch_shapes=[pltpu.VMEM((OUT_TM, D_MODEL), F32)],
        compiler_params=_params(("arbitrary",)),
        name="outproj",
    )(y_ssm, y_att, w_out, x2, g_post, g_pre)


MLP_TM = 512
MLP_TF = 2048
MLP_PART = 512


def _mlp_kernel(u_ref, wup_ref, wdown_ref, h_ref, g_ref, o_ref):
    f = pl.program_id(1)

    @pl.when(f == 0)
    def _():
        o_ref[...] = jnp.zeros_like(o_ref)

    for part in range(MLP_TF // MLP_PART):
        cols = slice(part * MLP_PART, (part + 1) * MLP_PART)
        hid = jnp.maximum(jnp.dot(u_ref[...], wup_ref[:, cols], preferred_element_type=F32), 0.0)
        o_ref[...] += jnp.dot((hid * hid).astype(BF16), wdown_ref[cols, :], preferred_element_type=F32)

    @pl.when(f == pl.num_programs(1) - 1)
    def _():
        chunk = 128

        def body(c, carry):
            sl = pl.ds(pl.multiple_of(c * chunk, chunk), chunk)
            o_ref[sl, :] = h_ref[sl, :] + _rms_scale(o_ref[sl, :], g_ref[...])
            return carry
        lax.fori_loop(0, MLP_TM // chunk, body, 0)


def _mlp(u2, w_up, w_down, h1, g_post):
    tokens = u2.shape[0]
    return pl.pallas_call(
        _mlp_kernel,
        grid=(tokens // MLP_TM, D_FF // MLP_TF),
        in_specs=[
            pl.BlockSpec((MLP_TM, D_MODEL), lambda i, f: (i, 0)),
            pl.BlockSpec((D_MODEL, MLP_TF), lambda i, f: (0, f)),
            pl.BlockSpec((MLP_TF, D_MODEL), lambda i, f: (f, 0)),
            pl.BlockSpec((MLP_TM, D_MODEL), lambda i, f: (i, 0)),
            pl.BlockSpec((1, D_MODEL), lambda i, f: (0, 0)),
        ],
        out_specs=pl.BlockSpec((MLP_TM, D_MODEL), lambda i, f: (i, 0)),
        out_shape=jax.ShapeDtypeStruct((tokens, D_MODEL), F32),
        compiler_params=_params(("arbitrary", "arbitrary")),
        name="mlp",
    )(u2, w_up, w_down, h1, g_post)


def kernel(x, norm_mix_pre, w_in, conv_w, conv_b, dt_bias, a_log, d_skip, ssm_norm_w, w_out,
           norm_mix_post, norm_mlp_pre, w_up, w_down, norm_mlp_post):
    batch, seq, _ = x.shape
    depth = w_in.shape[0]
    h = x.reshape(batch * seq, D_MODEL)
    for layer in range(depth):
        w_main, w_dt = _prep_in_weights(w_in[layer])
        zxbc, dt_raw, qkv = _in_projection(
            h, norm_mix_pre[layer].reshape(1, D_MODEL), w_main, w_dt, batch, seq)
        y_ssm = _ssd(zxbc, dt_raw, conv_w[layer], conv_b[layer], dt_bias[layer], a_log[layer],
                     d_skip[layer], ssm_norm_w[layer], batch, seq)
        y_att = _attention(qkv, batch, seq).reshape(batch * seq, D_ATT)
        h1, u2 = _out_projection(
            y_ssm, y_att, w_out[layer].astype(BF16), h,
            norm_mix_post[layer].reshape(1, D_MODEL), norm_mlp_pre[layer].reshape(1, D_MODEL))
        h = _mlp(u2, w_up[layer].astype(BF16), w_down[layer].astype(BF16), h1,
                 norm_mlp_post[layer].reshape(1, D_MODEL))
    return h.reshape(batch, seq, D_MODEL)
```

```python
import functools

import numpy as np
import jax
import jax.numpy as jnp
from jax import lax
from jax.experimental import pallas as pl
from jax.experimental.pallas import tpu as pltpu

F32 = jnp.float32
BF16 = jnp.bfloat16

D_MODEL = 2048
SSM_HEAD_DIM = 64
SSM_HEADS = 32
SSM_GROUPS = 8
HEADS_PER_GROUP = 4
D_STATE = 128
D_SSM = SSM_HEADS * SSM_HEAD_DIM
D_BC = 2 * SSM_GROUPS * D_STATE
GROUP_WIDTH = HEADS_PER_GROUP * SSM_HEAD_DIM
CONV_WIDTH = 4
CHUNK = 128
ATT_HEADS = 16
ATT_HEAD_DIM = 128
D_ATT = ATT_HEADS * ATT_HEAD_DIM
D_MIX = D_SSM + D_ATT
D_FF = 4 * D_MODEL
WINDOW_REACH = 128
MAX_DILATION = 16
EPS = 1e-6
ATT_SCALE = ATT_HEAD_DIM ** -0.5
MASKED = -1e30

LANES = 128
VMEM_LIMIT = 56 * 1024 * 1024


def _params(semantics):
    return pltpu.CompilerParams(dimension_semantics=semantics, vmem_limit_bytes=VMEM_LIMIT)


def _silu(v):
    return v * (1.0 / (1.0 + jnp.exp(-v)))


def _split3(v):
    hi = v.astype(BF16)
    rest = v - hi.astype(F32)
    mid = rest.astype(BF16)
    lo = (rest - mid.astype(F32)).astype(BF16)
    return hi, mid, lo


def _rms_scale(v, gain):
    ms = jnp.mean(v * v, axis=-1, keepdims=True)
    return (v * lax.rsqrt(ms + EPS)) * gain


PREP_COLS = 512


def _prep_kernel(wt_ref, wdt_ref, main_ref, dt_ref):
    main_ref[...] = wt_ref[...].T.astype(BF16)

    @pl.when(pl.program_id(0) == 0)
    def _():
        padded = jnp.concatenate([wdt_ref[...], jnp.zeros((LANES - SSM_HEADS, D_MODEL), F32)], axis=0)
        dt_ref[...] = padded.T.astype(BF16)


def _prep_in_weights(w):
    wt = w.T
    ssm_cols = 2 * D_SSM + D_BC
    att0 = ssm_cols + SSM_HEADS
    main_cols = ssm_cols + 3 * D_ATT
    n_ssm = ssm_cols // PREP_COLS

    def feature_row(j):
        return pl.multiple_of(jnp.where(j < n_ssm, j * PREP_COLS, att0 + (j - n_ssm) * PREP_COLS), 8)

    return pl.pallas_call(
        _prep_kernel,
        grid=(main_cols // PREP_COLS,),
        in_specs=[
            pl.BlockSpec((pl.Element(PREP_COLS), pl.Element(D_MODEL)), lambda j: (feature_row(j), 0)),
            pl.BlockSpec((pl.Element(SSM_HEADS), pl.Element(D_MODEL)), lambda j: (ssm_cols, 0)),
        ],
        out_specs=[pl.BlockSpec((D_MODEL, PREP_COLS), lambda j: (0, j)),
                   pl.BlockSpec((D_MODEL, LANES), lambda j: (0, 0))],
        out_shape=[jax.ShapeDtypeStruct((D_MODEL, main_cols), BF16),
                   jax.ShapeDtypeStruct((D_MODEL, LANES), BF16)],
        compiler_params=_params(("arbitrary",)),
        name="inproj_weights",
    )(wt, wt)


IN_TM = 1024
IN_TN = 1024
IN_STAGE_SLABS = 4
IN_STAGE_PITCH = 24


def _inproj_kernel(x_ref, g_ref, w_ref, wdt_ref, zxbc_ref, dt_ref, qkv_ref, u_ref, uperm_ref, uf_ref,
                   *, n_ssm_tiles):
    j = pl.program_id(1)
    chunk = 128
    n_slabs = D_MODEL // LANES
    rows = IN_TM // MAX_DILATION

    @pl.when(j == 0)
    def _():
        def body(c, carry):
            sl = pl.ds(pl.multiple_of(c * chunk, chunk), chunk)
            u_ref[sl, :] = _rms_scale(x_ref[sl, :], g_ref[...]).astype(BF16)
            return carry
        lax.fori_loop(0, IN_TM // chunk, body, 0)
        dt_ref[...] = jnp.dot(u_ref[...], wdt_ref[...], preferred_element_type=F32)
        for s0 in range(0, n_slabs, IN_STAGE_SLABS):
            for s in range(IN_STAGE_SLABS):
                cols = slice((s0 + s) * LANES, (s0 + s + 1) * LANES)
                for grp in range(rows):
                    uf_ref[s, grp * IN_STAGE_PITCH:grp * IN_STAGE_PITCH + MAX_DILATION, :] = (
                        u_ref[grp * MAX_DILATION:(grp + 1) * MAX_DILATION, cols].astype(F32))
            for r in range(MAX_DILATION):
                for s in range(IN_STAGE_SLABS):
                    uperm_ref[r * rows:(r + 1) * rows, (s0 + s) * LANES:(s0 + s + 1) * LANES] = (
                        uf_ref[s, pl.ds(r, rows, stride=IN_STAGE_PITCH), :].astype(BF16))

    @pl.when(j < n_ssm_tiles)
    def _():
        zxbc_ref[...] = jnp.dot(u_ref[...], w_ref[...], preferred_element_type=F32)

    @pl.when(j >= n_ssm_tiles)
    def _():
        scale = jnp.where(j < n_ssm_tiles + D_ATT // IN_TN, ATT_SCALE, 1.0).astype(F32)
        res = jnp.dot(uperm_ref[...], w_ref[...], preferred_element_type=F32) * scale
        for r in range(MAX_DILATION):
            qkv_ref[r] = res[r * rows:(r + 1) * rows, :].astype(BF16)


def _in_projection(x2, gain, w_main, w_dt, batch, seq):
    tokens = x2.shape[0]
    n_m = tokens // IN_TM
    m_per_seq = seq // IN_TM
    ssm_cols = 2 * D_SSM + D_BC
    att_cols = 3 * D_ATT
    n_ssm_tiles = ssm_cols // IN_TN
    n_att_tiles = att_cols // IN_TN
    run_len = seq // MAX_DILATION
    return pl.pallas_call(
        functools.partial(_inproj_kernel, n_ssm_tiles=n_ssm_tiles),
        grid=(n_m, n_ssm_tiles + n_att_tiles),
        in_specs=[
            pl.BlockSpec((IN_TM, D_MODEL), lambda i, j: (i, 0)),
            pl.BlockSpec((1, D_MODEL), lambda i, j: (0, 0)),
            pl.BlockSpec((D_MODEL, IN_TN), lambda i, j: (0, j)),
            pl.BlockSpec((D_MODEL, LANES), lambda i, j: (0, 0)),
        ],
        out_specs=[
            pl.BlockSpec((IN_TM, IN_TN), lambda i, j: (i, jnp.minimum(j, n_ssm_tiles - 1))),
            pl.BlockSpec((IN_TM, LANES), lambda i, j: (i, 0)),
            pl.BlockSpec((None, MAX_DILATION, IN_TM // MAX_DILATION, IN_TN),
                         lambda i, j: (i // m_per_seq, 0, i % m_per_seq, jnp.maximum(j - n_ssm_tiles, 0))),
        ],
        out_shape=[
            jax.ShapeDtypeStruct((tokens, ssm_cols), F32),
            jax.ShapeDtypeStruct((tokens, LANES), F32),
            jax.ShapeDtypeStruct((batch, MAX_DILATION, run_len, att_cols), BF16),
        ],
        scratch_shapes=[
            pltpu.VMEM((IN_TM, D_MODEL), BF16),
            pltpu.VMEM((IN_TM, D_MODEL), BF16),
            pltpu.VMEM((IN_STAGE_SLABS, IN_TM // MAX_DILATION * IN_STAGE_PITCH, LANES), F32),
        ],
        compiler_params=_params(("arbitrary", "arbitrary")),
        name="inproj",
    )(x2, gain, w_main, w_dt)


CONV_PAD = 8
SSD_ROWS = 2 * CHUNK


def _ssd_kernel(z_ref, x_ref, bc_ref, dt_ref, cwx_ref, cbx_ref, cwbc_ref, cbbc_ref,
                dtb_ref, alog_ref, dsk_ref, nw_ref, tri_ref, expand_ref,
                y_ref, state_ref, xpad_ref, bcpad_ref):
    @pl.when(pl.program_id(1) == 0)
    def _():
        state_ref[...] = jnp.zeros_like(state_ref)
        xpad_ref[0:CONV_PAD, :] = jnp.zeros((CONV_PAD, D_SSM), F32)
        bcpad_ref[0:CONV_PAD, :] = jnp.zeros((CONV_PAD, D_BC), F32)

    xpad_ref[CONV_PAD:CONV_PAD + SSD_ROWS, :] = x_ref[...]
    bcpad_ref[CONV_PAD:CONV_PAD + SSD_ROWS, :] = bc_ref[...]

    def conv_silu(pad_ref, w_ref, b_ref, r0, c0, width):
        cols = slice(c0, c0 + width)
        acc = b_ref[:, cols]
        for k in range(CONV_WIDTH):
            start = r0 + CONV_PAD - (CONV_WIDTH - 1) + k
            acc = acc + w_ref[k:k + 1, cols] * pad_ref[start:start + CHUNK, cols]
        return _silu(acc)

    tri = tri_ref[...]
    expand = expand_ref[...]
    row_i = lax.broadcasted_iota(jnp.int32, (CHUNK, CHUNK), 0)
    col_j = lax.broadcasted_iota(jnp.int32, (CHUNK, CHUNK), 1)
    causal = row_i >= col_j
    lane_head = lax.broadcasted_iota(jnp.int32, (CHUNK, GROUP_WIDTH), 1) // SSM_HEAD_DIM
    own_lanes = [jnp.where(lane_head == r, 1.0, 0.0).astype(BF16) for r in range(HEADS_PER_GROUP)]

    for r0 in range(0, SSD_ROWS, CHUNK):
        rows = slice(r0, r0 + CHUNK)
        dt_in = dt_ref[rows, :] + dtb_ref[...]
        dtv = jnp.maximum(dt_in, 0.0) + jnp.log1p(jnp.exp(-jnp.abs(dt_in)))
        da = dtv * (-jnp.exp(alog_ref[...]))
        a_cs = sum(jnp.dot(tri, part, preferred_element_type=F32)
                   for part in _split3(da))
        a_last = a_cs[CHUNK - 1:CHUNK, :]
        w_end = dtv * jnp.exp(a_last - a_cs)
        a_cs_t = a_cs.T
        dt_t = dtv.T
        w_end_t = w_end.T
        chunk_decay = sum(jnp.dot(part, expand, preferred_element_type=F32)
                          for part in _split3(jnp.broadcast_to(jnp.exp(a_last), (8, LANES)))
                          )[0:1, :]

        for g in range(SSM_GROUPS):
            gcols = slice(g * GROUP_WIDTH, (g + 1) * GROUP_WIDTH)
            xs = conv_silu(xpad_ref, cwx_ref, cbx_ref, r0, g * GROUP_WIDTH, GROUP_WIDTH)
            bm = conv_silu(bcpad_ref, cwbc_ref, cbbc_ref, r0, g * D_STATE, D_STATE)
            cm = conv_silu(bcpad_ref, cwbc_ref, cbbc_ref, r0, D_BC // 2 + g * D_STATE, D_STATE)
            cb = lax.dot_general(cm.astype(BF16), bm.astype(BF16), (((1,), (1,)), ((), ())),
                                 preferred_element_type=F32)
            bm_t = bm.T
            prev = state_ref[g]
            xs16 = xs.astype(BF16)
            prev16 = prev.astype(BF16)

            lhs_y, rhs_y, lhs_s, rhs_s = [], [], [], []
            for r in range(HEADS_PER_GROUP):
                h = g * HEADS_PER_GROUP + r
                col_a = jnp.broadcast_to(a_cs[:, h:h + 1], (CHUNK, CHUNK))
                row_a = a_cs_t[h:h + 1, :]
                decay = jnp.exp(jnp.where(causal, col_a - row_a, -jnp.inf))
                lhs_y.append((cb * decay * dt_t[h:h + 1, :]).astype(BF16))
                lhs_y.append((cm * jnp.exp(col_a)).astype(BF16))
                x_own = xs16 * own_lanes[r]
                rhs_y.append(x_own)
                rhs_y.append(prev16 * own_lanes[r])
                lhs_s.append((bm_t * w_end_t[h:h + 1, :]).astype(BF16))
                rhs_s.append(x_own)

            y = jnp.dot(jnp.concatenate(lhs_y, axis=1), jnp.concatenate(rhs_y, axis=0),
                        preferred_element_type=F32)
            s_new = jnp.dot(jnp.concatenate(lhs_s, axis=1), jnp.concatenate(rhs_s, axis=0),
                            preferred_element_type=F32)
            state_ref[g] = prev * chunk_decay[:, gcols] + s_new

            y = y + dsk_ref[:, gcols] * xs
            gated = y * _silu(z_ref[rows, gcols])
            y_ref[rows, gcols] = _rms_scale(gated, nw_ref[:, gcols]).astype(BF16)

    xpad_ref[0:CONV_PAD, :] = x_ref[SSD_ROWS - CONV_PAD:SSD_ROWS, :]
    bcpad_ref[0:CONV_PAD, :] = bc_ref[SSD_ROWS - CONV_PAD:SSD_ROWS, :]


def _ssd(zxbc, dt_raw, conv_w, conv_b, dt_bias, a_log, d_skip, ssm_norm_w, batch, seq):
    tokens = zxbc.shape[0]
    n_steps = seq // SSD_ROWS
    pad_heads = LANES - SSM_HEADS
    dtb = jnp.pad(dt_bias.astype(F32), (0, pad_heads)).reshape(1, LANES)
    alog = jnp.pad(a_log.astype(F32), (0, pad_heads)).reshape(1, LANES)
    dsk = jnp.repeat(d_skip.astype(F32), SSM_HEAD_DIM).reshape(1, D_SSM)
    tri = jnp.asarray(np.tril(np.ones((CHUNK, CHUNK), np.float32)), dtype=BF16)
    expand = np.zeros((LANES, D_SSM), np.float32)
    for h in range(SSM_HEADS):
        expand[h, h * SSM_HEAD_DIM:(h + 1) * SSM_HEAD_DIM] = 1.0
    expand = jnp.asarray(expand, dtype=BF16)

    def rows(b, c):
        return b * n_steps + c

    def full(shape):
        return pl.BlockSpec(shape, lambda b, c: (0,) * len(shape))

    return pl.pallas_call(
        _ssd_kernel,
        grid=(batch, n_steps),
        in_specs=[
            pl.BlockSpec((SSD_ROWS, D_SSM), lambda b, c: (rows(b, c), 0)),
            pl.BlockSpec((SSD_ROWS, D_SSM), lambda b, c: (rows(b, c), 1)),
            pl.BlockSpec((SSD_ROWS, D_BC), lambda b, c: (rows(b, c), 2)),
            pl.BlockSpec((SSD_ROWS, LANES), lambda b, c: (rows(b, c), 0)),
            full((CONV_WIDTH, D_SSM)), full((1, D_SSM)),
            full((CONV_WIDTH, D_BC)), full((1, D_BC)),
            full((1, LANES)), full((1, LANES)), full((1, D_SSM)), full((1, D_SSM)),
            full((CHUNK, CHUNK)), full((LANES, D_SSM)),
        ],
        out_specs=pl.BlockSpec((SSD_ROWS, D_SSM), lambda b, c: (rows(b, c), 0)),
        out_shape=jax.ShapeDtypeStruct((tokens, D_SSM), BF16),
        scratch_shapes=[
            pltpu.VMEM((SSM_GROUPS, D_STATE, GROUP_WIDTH), F32),
            pltpu.VMEM((CONV_PAD + SSD_ROWS, D_SSM), F32),
            pltpu.VMEM((CONV_PAD + SSD_ROWS, D_BC), F32),
        ],
        compiler_params=_params(("arbitrary", "arbitrary")),
        name="ssd",
    )(zxbc, zxbc, zxbc, dt_raw,
      conv_w[:, :D_SSM], conv_b[:D_SSM].reshape(1, D_SSM),
      conv_w[:, D_SSM:], conv_b[D_SSM:].reshape(1, D_BC),
      dtb, alog, dsk, ssm_norm_w.reshape(1, D_SSM), tri, expand)


BRANCHES = ((16, 16, 16), (4, 32, 32), (1, 128, 128))
ATT_GROUP = (3, 6, 6)


def _branch_bias(n_runs, width, back):
    s_q = np.repeat(np.arange(n_runs), width)
    i_q = np.tile(np.arange(width), n_runs)
    j_q = n_runs * i_q + s_q
    s_p = np.repeat(np.arange(n_runs), back)
    i_p = np.tile(np.arange(back), n_runs) - back
    j_k = np.concatenate([n_runs * i_p + s_p, j_q])
    dist = j_q[:, None] - j_k[None, :]
    return np.where((dist >= 0) & (dist <= WINDOW_REACH), 0.0, MASKED).astype(np.float32)


def _attn_kernel(q_ref, k_ref, v_ref, b0_ref, b1_ref, b2_ref, o_ref,
                 num_ref, m_ref, l_ref, nat_ref, s0_ref, p0_ref, s1_ref, p1_ref, *, run_len):
    bias_refs = (b0_ref, b1_ref, b2_ref)
    stage_refs = ((s0_ref, p0_ref), (s1_ref, p1_ref), (s1_ref, p1_ref))

    def rows_of(ref, starts, size):
        parts = [ref[st:st + size, :] for st in starts]
        return parts[0] if len(parts) == 1 else jnp.concatenate(parts, axis=0)

    def keys_of(ref, branch, starts, with_prev):
        n_runs, width, back = BRANCHES[branch]
        cur = rows_of(ref, starts, width)
        if not with_prev:
            return cur
        return jnp.concatenate([rows_of(ref, [st - back for st in starts], back), cur], axis=0)

    def layout(branch, blocks, slot):
        n_runs, width, back = BRANCHES[branch]
        n_cls = MAX_DILATION // n_runs
        out = []
        for u, (cls, i0, with_prev) in enumerate(blocks):
            starts = [(cls + n_cls * s) * run_len + i0 for s in range(n_runs)]
            out.append((slot * ATT_GROUP[branch] + u, starts, with_prev,
                        n_runs * (width + (back if with_prev else 0))))
        return out

    def issue(branch, blocks, slot):
        n_runs, width, back = BRANCHES[branch]
        s_ref, _ = stage_refs[branch]
        for idx, starts, with_prev, n_keys in layout(branch, blocks, slot):
            q = rows_of(q_ref, starts, width)
            k = keys_of(k_ref, branch, starts, with_prev)
            bias = bias_refs[branch][...] if with_prev else bias_refs[branch][:, n_runs * back:]
            s_ref[idx, :, 0:n_keys] = lax.dot_general(
                q, k, (((1,), (1,)), ((), ())), preferred_element_type=F32) + bias

    def finish(branch, blocks, slot):
        n_runs, width, back = BRANCHES[branch]
        s_ref, p_ref = stage_refs[branch]
        plan = layout(branch, blocks, slot)
        for idx, starts, with_prev, n_keys in plan:
            s = s_ref[idx, :, 0:n_keys]
            m = jnp.max(s, axis=-1, keepdims=True)
            p_ref[idx, :, 0:n_keys] = jnp.exp(s - m).astype(BF16)
            for part, st in enumerate(starts):
                m_ref[branch, st:st + width, :] = jnp.broadcast_to(
                    m[part * width:(part + 1) * width], (width, ATT_HEAD_DIM))
        for idx, starts, with_prev, n_keys in plan:
            v = keys_of(v_ref, branch, starts, with_prev)
            v1 = jnp.concatenate([v, jnp.ones((n_keys, ATT_HEAD_DIM), BF16)], axis=1)
            pv = jnp.dot(p_ref[idx, :, 0:n_keys], v1, preferred_element_type=F32)
            for part, st in enumerate(starts):
                rows = slice(part * width, (part + 1) * width)
                num_ref[branch, st:st + width, :] = pv[rows, 0:ATT_HEAD_DIM]
                l_ref[branch, st:st + width, :] = pv[rows, ATT_HEAD_DIM:]

    groups = []
    for branch, (n_runs, width, back) in enumerate(BRANCHES):
        blocks = [(cls, i0, i0 > 0) for cls in range(MAX_DILATION // n_runs) for i0 in range(0, run_len, width)]
        size = ATT_GROUP[branch]
        groups += [(branch, blocks[g:g + size]) for g in range(0, len(blocks), size)]
    issue(*groups[0], 0)
    for n, group in enumerate(groups):
        if n + 1 < len(groups):
            issue(*groups[n + 1], (n + 1) % 2)
        finish(*group, n % 2)

    def combine(r, carry):
        sl = pl.ds(pl.multiple_of(r * run_len, run_len), run_len)
        m0, m1, m2 = m_ref[0, sl, :], m_ref[1, sl, :], m_ref[2, sl, :]
        top = jnp.maximum(jnp.maximum(m0, m1), m2)
        w0, w1, w2 = jnp.exp(m0 - top), jnp.exp(m1 - top), jnp.exp(m2 - top)
        den = w0 * l_ref[0, sl, :] + w1 * l_ref[1, sl, :] + w2 * l_ref[2, sl, :]
        num = w0 * num_ref[0, sl, :] + w1 * num_ref[1, sl, :] + w2 * num_ref[2, sl, :]
        nat_ref[pl.ds(r, run_len, stride=MAX_DILATION), :] = num / den
        return carry
    lax.fori_loop(0, MAX_DILATION, combine, 0)

    rows = 256

    def emit(c, carry):
        sl = pl.ds(pl.multiple_of(c * rows, rows), rows)
        o_ref[sl, :] = nat_ref[sl, :].astype(BF16)
        return carry
    lax.fori_loop(0, (MAX_DILATION * run_len) // rows, emit, 0)


def _stage_scratch():
    out = []
    for branch in (0, 1):
        n_runs, width, back = BRANCHES[branch]
        shape = (2 * ATT_GROUP[branch], n_runs * width, n_runs * (width + back))
        out += [pltpu.VMEM(shape, F32), pltpu.VMEM(shape, BF16)]
    return out


def _attention(qkv, batch, seq):
    run_len = seq // MAX_DILATION
    qkv2 = qkv.reshape(batch, seq, 3 * D_ATT)
    biases = [jnp.asarray(_branch_bias(*b)) for b in BRANCHES]

    def head_block(offset):
        return pl.BlockSpec((None, seq, ATT_HEAD_DIM), lambda b, h: (b, 0, offset + h))

    def full(shape):
        return pl.BlockSpec(shape, lambda b, h: (0,) * len(shape))

    return pl.pallas_call(
        functools.partial(_attn_kernel, run_len=run_len),
        grid=(batch, ATT_HEADS),
        in_specs=[head_block(0), head_block(ATT_HEADS), head_block(2 * ATT_HEADS)]
        + [full(b.shape) for b in biases],
        out_specs=pl.BlockSpec((None, seq, ATT_HEAD_DIM), lambda b, h: (b, 0, h)),
        out_shape=jax.ShapeDtypeStruct((batch, seq, D_ATT), BF16),
        scratch_shapes=[
            pltpu.VMEM((len(BRANCHES), seq, ATT_HEAD_DIM), F32),
            pltpu.VMEM((len(BRANCHES), seq, ATT_HEAD_DIM), F32),
            pltpu.VMEM((len(BRANCHES), seq, ATT_HEAD_DIM), F32),
            pltpu.VMEM((seq, ATT_HEAD_DIM), F32),
        ] + _stage_scratch(),
        compiler_params=_params(("arbitrary", "arbitrary")),
        name="dilated_attn",
    )(qkv2, qkv2, qkv2, *biases)


OUT_TM = 512


def _outproj_kernel(ys_ref, ya_ref, w_ref, x_ref, gpost_ref, gpre_ref, h_ref, u_ref, mix_ref):
    mix_ref[...] = jnp.dot(ys_ref[...], w_ref[0:D_SSM, :], preferred_element_type=F32)
    mix_ref[...] += jnp.dot(ya_ref[...], w_ref[D_SSM:D_MIX, :], preferred_element_type=F32)
    chunk = 128

    def body(c, carry):
        sl = pl.ds(pl.multiple_of(c * chunk, chunk), chunk)
        h = x_ref[sl, :] + _rms_scale(mix_ref[sl, :], gpost_ref[...])
        h_ref[sl, :] = h
        u_ref[sl, :] = _rms_scale(h, gpre_ref[...]).astype(BF16)
        return carry
    lax.fori_loop(0, OUT_TM // chunk, body, 0)


def _out_projection(y_ssm, y_att, w_out, x2, g_post, g_pre):
    tokens = x2.shape[0]

    def rows(width):
        return pl.BlockSpec((OUT_TM, width), lambda i: (i, 0))

    def whole(shape):
        return pl.BlockSpec(shape, lambda i: (0, 0), pipeline_mode=pl.Buffered(1))

    return pl.pallas_call(
        _outproj_kernel,
        grid=(tokens // OUT_TM,),
        in_specs=[rows(D_SSM), rows(D_ATT), whole((D_MIX, D_MODEL)), rows(D_MODEL),
                  whole((1, D_MODEL)), whole((1, D_MODEL))],
        out_specs=[rows(D_MODEL), rows(D_MODEL)],
        out_shape=[
            jax.ShapeDtypeStruct((tokens, D_MODEL), F32),
            jax.ShapeDtypeStruct((tokens, D_MODEL), BF16),
        ],
        scratch_shapes=[pltpu.VMEM((OUT_TM, D_MODEL), F32)],
        compiler_params=_params(("arbitrary",)),
        name="outproj",
    )(y_ssm, y_att, w_out, x2, g_post, g_pre)


MLP_TM = 512
MLP_TF = 2048
MLP_PART = 512


def _mlp_kernel(u_ref, wup_ref, wdown_ref, h_ref, g_ref, o_ref):
    f = pl.program_id(1)

    @pl.when(f == 0)
    def _():
        o_ref[...] = jnp.zeros_like(o_ref)

    for part in range(MLP_TF // MLP_PART):
        cols = slice(part * MLP_PART, (part + 1) * MLP_PART)
        hid = jnp.maximum(jnp.dot(u_ref[...], wup_ref[:, cols], preferred_element_type=F32), 0.0)
        o_ref[...] += jnp.dot((hid * hid).astype(BF16), wdown_ref[cols, :], preferred_element_type=F32)

    @pl.when(f == pl.num_programs(1) - 1)
    def _():
        chunk = 128

        def body(c, carry):
            sl = pl.ds(pl.multiple_of(c * chunk, chunk), chunk)
            o_ref[sl, :] = h_ref[sl, :] + _rms_scale(o_ref[sl, :], g_ref[...])
            return carry
        lax.fori_loop(0, MLP_TM // chunk, body, 0)


def _mlp(u2, w_up, w_down, h1, g_post):
    tokens = u2.shape[0]
    return pl.pallas_call(
        _mlp_kernel,
        grid=(tokens // MLP_TM, D_FF // MLP_TF),
        in_specs=[
            pl.BlockSpec((MLP_TM, D_MODEL), lambda i, f: (i, 0)),
            pl.BlockSpec((D_MODEL, MLP_TF), lambda i, f: (0, f)),
            pl.BlockSpec((MLP_TF, D_MODEL), lambda i, f: (f, 0)),
            pl.BlockSpec((MLP_TM, D_MODEL), lambda i, f: (i, 0)),
            pl.BlockSpec((1, D_MODEL), lambda i, f: (0, 0)),
        ],
        out_specs=pl.BlockSpec((MLP_TM, D_MODEL), lambda i, f: (i, 0)),
        out_shape=jax.ShapeDtypeStruct((tokens, D_MODEL), F32),
        compiler_params=_params(("arbitrary", "arbitrary")),
        name="mlp",
    )(u2, w_up, w_down, h1, g_post)


def kernel(x, norm_mix_pre, w_in, conv_w, conv_b, dt_bias, a_log, d_skip, ssm_norm_w, w_out,
           norm_mix_post, norm_mlp_pre, w_up, w_down, norm_mlp_post):
    batch, seq, _ = x.shape
    depth = w_in.shape[0]
    h = x.reshape(batch * seq, D_MODEL)
    for layer in range(depth):
        w_main, w_dt = _prep_in_weights(w_in[layer])
        zxbc, dt_raw, qkv = _in_projection(
            h, norm_mix_pre[layer].reshape(1, D_MODEL), w_main, w_dt, batch, seq)
        y_ssm = _ssd(zxbc, dt_raw, conv_w[layer], conv_b[layer], dt_bias[layer], a_log[layer],
                     d_skip[layer], ssm_norm_w[layer], batch, seq)
        y_att = _attention(qkv, batch, seq).reshape(batch * seq, D_ATT)
        h1, u2 = _out_projection(
            y_ssm, y_att, w_out[layer].astype(BF16), h,
            norm_mix_post[layer].reshape(1, D_MODEL), norm_mlp_pre[layer].reshape(1, D_MODEL))
        h = _mlp(u2, w_up[layer].astype(BF16), w_down[layer].astype(BF16), h1,
                 norm_mlp_post[layer].reshape(1, D_MODEL))
    return h.reshape(batch, seq, D_MODEL)
```

```python
import functools

import numpy as np
import jax
import jax.numpy as jnp
from jax import lax
from jax.experimental import pallas as pl
from jax.experimental.pallas import tpu as pltpu

F32 = jnp.float32
BF16 = jnp.bfloat16

D_MODEL = 2048
SSM_HEAD_DIM = 64
SSM_HEADS = 32
SSM_GROUPS = 8
HEADS_PER_GROUP = 4
D_STATE = 128
D_SSM = SSM_HEADS * SSM_HEAD_DIM
D_BC = 2 * SSM_GROUPS * D_STATE
GROUP_WIDTH = HEADS_PER_GROUP * SSM_HEAD_DIM
CONV_WIDTH = 4
CHUNK = 128
ATT_HEADS = 16
ATT_HEAD_DIM = 128
D_ATT = ATT_HEADS * ATT_HEAD_DIM
D_MIX = D_SSM + D_ATT
D_FF = 4 * D_MODEL
WINDOW_REACH = 128
MAX_DILATION = 16
EPS = 1e-6
ATT_SCALE = ATT_HEAD_DIM ** -0.5
MASKED = -1e30

LANES = 128
VMEM_LIMIT = 56 * 1024 * 1024


def _params(semantics):
    return pltpu.CompilerParams(dimension_semantics=semantics, vmem_limit_bytes=VMEM_LIMIT)


def _silu(v):
    return v * (1.0 / (1.0 + jnp.exp(-v)))


def _split3(v):
    hi = v.astype(BF16)
    rest = v - hi.astype(F32)
    mid = rest.astype(BF16)
    lo = (rest - mid.astype(F32)).astype(BF16)
    return hi, mid, lo


def _rms_scale(v, gain):
    ms = jnp.mean(v * v, axis=-1, keepdims=True)
    return (v * lax.rsqrt(ms + EPS)) * gain


PREP_COLS = 512


def _prep_kernel(wt_ref, wdt_ref, main_ref, dt_ref):
    main_ref[...] = wt_ref[...].T.astype(BF16)

    @pl.when(pl.program_id(0) == 0)
    def _():
        padded = jnp.concatenate([wdt_ref[...], jnp.zeros((LANES - SSM_HEADS, D_MODEL), F32)], axis=0)
        dt_ref[...] = padded.T.astype(BF16)


def _prep_in_weights(w):
    wt = w.T
    ssm_cols = 2 * D_SSM + D_BC
    att0 = ssm_cols + SSM_HEADS
    main_cols = ssm_cols + 3 * D_ATT
    n_ssm = ssm_cols // PREP_COLS

    def feature_row(j):
        return pl.multiple_of(jnp.where(j < n_ssm, j * PREP_COLS, att0 + (j - n_ssm) * PREP_COLS), 8)

    return pl.pallas_call(
        _prep_kernel,
        grid=(main_cols // PREP_COLS,),
        in_specs=[
            pl.BlockSpec((pl.Element(PREP_COLS), pl.Element(D_MODEL)), lambda j: (feature_row(j), 0)),
            pl.BlockSpec((pl.Element(SSM_HEADS), pl.Element(D_MODEL)), lambda j: (ssm_cols, 0)),
        ],
        out_specs=[pl.BlockSpec((D_MODEL, PREP_COLS), lambda j: (0, j)),
                   pl.BlockSpec((D_MODEL, LANES), lambda j: (0, 0))],
        out_shape=[jax.ShapeDtypeStruct((D_MODEL, main_cols), BF16),
                   jax.ShapeDtypeStruct((D_MODEL, LANES), BF16)],
        compiler_params=_params(("arbitrary",)),
        name="inproj_weights",
    )(wt, wt)


IN_TM = 1024
IN_TN = 1024
IN_STAGE_SLABS = 4
IN_STAGE_PITCH = 24


def _inproj_kernel(x_ref, g_ref, w_ref, wdt_ref, zxbc_ref, dt_ref, qkv_ref, u_ref, uperm_ref, uf_ref,
                   *, n_ssm_tiles):
    j = pl.program_id(1)
    chunk = 128
    n_slabs = D_MODEL // LANES
    rows = IN_TM // MAX_DILATION

    @pl.when(j == 0)
    def _():
        def body(c, carry):
            sl = pl.ds(pl.multiple_of(c * chunk, chunk), chunk)
            u_ref[sl, :] = _rms_scale(x_ref[sl, :], g_ref[...]).astype(BF16)
            return carry
        lax.fori_loop(0, IN_TM // chunk, body, 0)
        dt_ref[...] = jnp.dot(u_ref[...], wdt_ref[...], preferred_element_type=F32)
        for s0 in range(0, n_slabs, IN_STAGE_SLABS):
            for s in range(IN_STAGE_SLABS):
                cols = slice((s0 + s) * LANES, (s0 + s + 1) * LANES)
                for grp in range(rows):
                    uf_ref[s, grp * IN_STAGE_PITCH:grp * IN_STAGE_PITCH + MAX_DILATION, :] = (
                        u_ref[grp * MAX_DILATION:(grp + 1) * MAX_DILATION, cols].astype(F32))
            for r in range(MAX_DILATION):
                for s in range(IN_STAGE_SLABS):
                    uperm_ref[r * rows:(r + 1) * rows, (s0 + s) * LANES:(s0 + s + 1) * LANES] = (
                        uf_ref[s, pl.ds(r, rows, stride=IN_STAGE_PITCH), :].astype(BF16))

    @pl.when(j < n_ssm_tiles)
    def _():
        zxbc_ref[...] = jnp.dot(u_ref[...], w_ref[...], preferred_element_type=F32)

    @pl.when(j >= n_ssm_tiles)
    def _():
        scale = jnp.where(j < n_ssm_tiles + D_ATT // IN_TN, ATT_SCALE, 1.0).astype(F32)
        res = jnp.dot(uperm_ref[...], w_ref[...], preferred_element_type=F32) * scale
        for r in range(MAX_DILATION):
            qkv_ref[r] = res[r * rows:(r + 1) * rows, :].astype(BF16)


def _in_projection(x2, gain, w_main, w_dt, batch, seq):
    tokens = x2.shape[0]
    n_m = tokens // IN_TM
    m_per_seq = seq // IN_TM
    ssm_cols = 2 * D_SSM + D_BC
    att_cols = 3 * D_ATT
    n_ssm_tiles = ssm_cols // IN_TN
    n_att_tiles = att_cols // IN_TN
    run_len = seq // MAX_DILATION
    return pl.pallas_call(
        functools.partial(_inproj_kernel, n_ssm_tiles=n_ssm_tiles),
        grid=(n_m, n_ssm_tiles + n_att_tiles),
        in_specs=[
            pl.BlockSpec((IN_TM, D_MODEL), lambda i, j: (i, 0)),
            pl.BlockSpec((1, D_MODEL), lambda i, j: (0, 0)),
            pl.BlockSpec((D_MODEL, IN_TN), lambda i, j: (0, j)),
            pl.BlockSpec((D_MODEL, LANES), lambda i, j: (0, 0)),
        ],
        out_specs=[
            pl.BlockSpec((IN_TM, IN_TN), lambda i, j: (i, jnp.minimum(j, n_ssm_tiles - 1))),
            pl.BlockSpec((IN_TM, LANES), lambda i, j: (i, 0)),
            pl.BlockSpec((None, MAX_DILATION, IN_TM // MAX_DILATION, IN_TN),
                         lambda i, j: (i // m_per_seq, 0, i % m_per_seq, jnp.maximum(j - n_ssm_tiles, 0))),
        ],
        out_shape=[
            jax.ShapeDtypeStruct((tokens, ssm_cols), F32),
            jax.ShapeDtypeStruct((tokens, LANES), F32),
            jax.ShapeDtypeStruct((batch, MAX_DILATION, run_len, att_cols), BF16),
        ],
        scratch_shapes=[
            pltpu.VMEM((IN_TM, D_MODEL), BF16),
            pltpu.VMEM((IN_TM, D_MODEL), BF16),
            pltpu.VMEM((IN_STAGE_SLABS, IN_TM // MAX_DILATION * IN_STAGE_PITCH, LANES), F32),
        ],
        compiler_params=_params(("arbitrary", "arbitrary")),
        name="inproj",
    )(x2, gain, w_main, w_dt)


CONV_PAD = 8
SSD_ROWS = 4 * CHUNK


def _ssd_kernel(z_ref, x_ref, bc_ref, dt_ref, cwx_ref, cbx_ref, cwbc_ref, cbbc_ref,
                dtb_ref, alog_ref, dsk_ref, nw_ref, tri_ref, expand_ref,
                y_ref, state_ref, xpad_ref, bcpad_ref):
    @pl.when(pl.program_id(1) == 0)
    def _():
        state_ref[...] = jnp.zeros_like(state_ref)
        xpad_ref[0:CONV_PAD, :] = jnp.zeros((CONV_PAD, D_SSM), F32)
        bcpad_ref[0:CONV_PAD, :] = jnp.zeros((CONV_PAD, D_BC), F32)

    xpad_ref[CONV_PAD:CONV_PAD + SSD_ROWS, :] = x_ref[...]
    bcpad_ref[CONV_PAD:CONV_PAD + SSD_ROWS, :] = bc_ref[...]

    def conv_silu(pad_ref, w_ref, b_ref, r0, c0, width):
        cols = slice(c0, c0 + width)
        acc = b_ref[:, cols]
        for k in range(CONV_WIDTH):
            start = r0 + CONV_PAD - (CONV_WIDTH - 1) + k
            acc = acc + w_ref[k:k + 1, cols] * pad_ref[start:start + CHUNK, cols]
        return _silu(acc)

    tri = tri_ref[...]
    expand = expand_ref[...]
    row_i = lax.broadcasted_iota(jnp.int32, (CHUNK, CHUNK), 0)
    col_j = lax.broadcasted_iota(jnp.int32, (CHUNK, CHUNK), 1)
    causal = row_i >= col_j
    lane_head = lax.broadcasted_iota(jnp.int32, (CHUNK, GROUP_WIDTH), 1) // SSM_HEAD_DIM
    own_lanes = [jnp.where(lane_head == r, 1.0, 0.0).astype(BF16) for r in range(HEADS_PER_GROUP)]

    for r0 in range(0, SSD_ROWS, CHUNK):
        rows = slice(r0, r0 + CHUNK)
        dt_in = dt_ref[rows, :] + dtb_ref[...]
        dtv = jnp.maximum(dt_in, 0.0) + jnp.log1p(jnp.exp(-jnp.abs(dt_in)))
        da = dtv * (-jnp.exp(alog_ref[...]))
        a_cs = sum(jnp.dot(tri, part, preferred_element_type=F32)
                   for part in _split3(da))
        a_last = a_cs[CHUNK - 1:CHUNK, :]
        w_end = dtv * jnp.exp(a_last - a_cs)
        a_cs_t = a_cs.T
        dt_t = dtv.T
        w_end_t = w_end.T
        chunk_decay = sum(jnp.dot(part, expand, preferred_element_type=F32)
                          for part in _split3(jnp.broadcast_to(jnp.exp(a_last), (8, LANES)))
                          )[0:1, :]

        for g in range(SSM_GROUPS):
            gcols = slice(g * GROUP_WIDTH, (g + 1) * GROUP_WIDTH)
            xs = conv_silu(xpad_ref, cwx_ref, cbx_ref, r0, g * GROUP_WIDTH, GROUP_WIDTH)
            bm = conv_silu(bcpad_ref, cwbc_ref, cbbc_ref, r0, g * D_STATE, D_STATE)
            cm = conv_silu(bcpad_ref, cwbc_ref, cbbc_ref, r0, D_BC // 2 + g * D_STATE, D_STATE)
            cb = lax.dot_general(cm.astype(BF16), bm.astype(BF16), (((1,), (1,)), ((), ())),
                                 preferred_element_type=F32)
            bm_t = bm.T
            prev = state_ref[g]
            xs16 = xs.astype(BF16)
            prev16 = prev.astype(BF16)

            lhs_y, rhs_y, lhs_s, rhs_s = [], [], [], []
            for r in range(HEADS_PER_GROUP):
                h = g * HEADS_PER_GROUP + r
                col_a = jnp.broadcast_to(a_cs[:, h:h + 1], (CHUNK, CHUNK))
                row_a = a_cs_t[h:h + 1, :]
                decay = jnp.exp(jnp.where(causal, col_a - row_a, -jnp.inf))
                lhs_y.append((cb * decay * dt_t[h:h + 1, :]).astype(BF16))
                lhs_y.append((cm * jnp.exp(col_a)).astype(BF16))
                x_own = xs16 * own_lanes[r]
                rhs_y.append(x_own)
                rhs_y.append(prev16 * own_lanes[r])
                lhs_s.append((bm_t * w_end_t[h:h + 1, :]).astype(BF16))
                rhs_s.append(x_own)

            y = jnp.dot(jnp.concatenate(lhs_y, axis=1), jnp.concatenate(rhs_y, axis=0),
                        preferred_element_type=F32)
            s_new = jnp.dot(jnp.concatenate(lhs_s, axis=1), jnp.concatenate(rhs_s, axis=0),
                            preferred_element_type=F32)
            state_ref[g] = prev * chunk_decay[:, gcols] + s_new

            y = y + dsk_ref[:, gcols] * xs
            gated = y * _silu(z_ref[rows, gcols])
            y_ref[rows, gcols] = _rms_scale(gated, nw_ref[:, gcols]).astype(BF16)

    xpad_ref[0:CONV_PAD, :] = x_ref[SSD_ROWS - CONV_PAD:SSD_ROWS, :]
    bcpad_ref[0:CONV_PAD, :] = bc_ref[SSD_ROWS - CONV_PAD:SSD_ROWS, :]


def _ssd(zxbc, dt_raw, conv_w, conv_b, dt_bias, a_log, d_skip, ssm_norm_w, batch, seq):
    tokens = zxbc.shape[0]
    n_steps = seq // SSD_ROWS
    pad_heads = LANES - SSM_HEADS
    dtb = jnp.pad(dt_bias.astype(F32), (0, pad_heads)).reshape(1, LANES)
    alog = jnp.pad(a_log.astype(F32), (0, pad_heads)).reshape(1, LANES)
    dsk = jnp.repeat(d_skip.astype(F32), SSM_HEAD_DIM).reshape(1, D_SSM)
    tri = jnp.asarray(np.tril(np.ones((CHUNK, CHUNK), np.float32)), dtype=BF16)
    expand = np.zeros((LANES, D_SSM), np.float32)
    for h in range(SSM_HEADS):
        expand[h, h * SSM_HEAD_DIM:(h + 1) * SSM_HEAD_DIM] = 1.0
    expand = jnp.asarray(expand, dtype=BF16)

    def rows(b, c):
        return b * n_steps + c

    def full(shape):
        return pl.BlockSpec(shape, lambda b, c: (0,) * len(shape))

    return pl.pallas_call(
        _ssd_kernel,
        grid=(batch, n_steps),
        in_specs=[
            pl.BlockSpec((SSD_ROWS, D_SSM), lambda b, c: (rows(b, c), 0)),
            pl.BlockSpec((SSD_ROWS, D_SSM), lambda b, c: (rows(b, c), 1)),
            pl.BlockSpec((SSD_ROWS, D_BC), lambda b, c: (rows(b, c), 2)),
            pl.BlockSpec((SSD_ROWS, LANES), lambda b, c: (rows(b, c), 0)),
            full((CONV_WIDTH, D_SSM)), full((1, D_SSM)),
            full((CONV_WIDTH, D_BC)), full((1, D_BC)),
            full((1, LANES)), full((1, LANES)), full((1, D_SSM)), full((1, D_SSM)),
            full((CHUNK, CHUNK)), full((LANES, D_SSM)),
        ],
        out_specs=pl.BlockSpec((SSD_ROWS, D_SSM), lambda b, c: (rows(b, c), 0)),
        out_shape=jax.ShapeDtypeStruct((tokens, D_SSM), BF16),
        scratch_shapes=[
            pltpu.VMEM((SSM_GROUPS, D_STATE, GROUP_WIDTH), F32),
            pltpu.VMEM((CONV_PAD + SSD_ROWS, D_SSM), F32),
            pltpu.VMEM((CONV_PAD + SSD_ROWS, D_BC), F32),
        ],
        compiler_params=_params(("arbitrary", "arbitrary")),
        name="ssd",
    )(zxbc, zxbc, zxbc, dt_raw,
      conv_w[:, :D_SSM], conv_b[:D_SSM].reshape(1, D_SSM),
      conv_w[:, D_SSM:], conv_b[D_SSM:].reshape(1, D_BC),
      dtb, alog, dsk, ssm_norm_w.reshape(1, D_SSM), tri, expand)


BRANCHES = ((16, 16, 16), (4, 32, 32), (1, 128, 128))
ATT_GROUP = (3, 6, 6)
assert ATT_GROUP[1] == ATT_GROUP[2]


def _branch_bias(n_runs, width, back):
    s_q = np.repeat(np.arange(n_runs), width)
    i_q = np.tile(np.arange(width), n_runs)
    j_q = n_runs * i_q + s_q
    s_p = np.repeat(np.arange(n_runs), back)
    i_p = np.tile(np.arange(back), n_runs) - back
    j_k = np.concatenate([n_runs * i_p + s_p, j_q])
    dist = j_q[:, None] - j_k[None, :]
    return np.where((dist >= 0) & (dist <= WINDOW_REACH), 0.0, MASKED).astype(np.float32)


def _attn_kernel(q_ref, k_ref, v_ref, b0_ref, b1_ref, b2_ref, o_ref,
                 num_ref, m_ref, l_ref, nat_ref, s0_ref, p0_ref, s1_ref, p1_ref, *, run_len):
    bias_refs = (b0_ref, b1_ref, b2_ref)
    stage_refs = ((s0_ref, p0_ref), (s1_ref, p1_ref), (s1_ref, p1_ref))

    def rows_of(ref, starts, size):
        parts = [ref[st:st + size, :] for st in starts]
        return parts[0] if len(parts) == 1 else jnp.concatenate(parts, axis=0)

    def keys_of(ref, branch, starts, with_prev):
        n_runs, width, back = BRANCHES[branch]
        cur = rows_of(ref, starts, width)
        if not with_prev:
            return cur
        return jnp.concatenate([rows_of(ref, [st - back for st in starts], back), cur], axis=0)

    def layout(branch, blocks, slot):
        n_runs, width, back = BRANCHES[branch]
        n_cls = MAX_DILATION // n_runs
        out = []
        for u, (cls, i0, with_prev) in enumerate(blocks):
            starts = [(cls + n_cls * s) * run_len + i0 for s in range(n_runs)]
            out.append((slot * ATT_GROUP[branch] + u, starts, with_prev,
                        n_runs * (width + (back if with_prev else 0))))
        return out

    def issue(branch, blocks, slot):
        n_runs, width, back = BRANCHES[branch]
        s_ref, _ = stage_refs[branch]
        for idx, starts, with_prev, n_keys in layout(branch, blocks, slot):
            q = rows_of(q_ref, starts, width)
            k = keys_of(k_ref, branch, starts, with_prev)
            bias = bias_refs[branch][...] if with_prev else bias_refs[branch][:, n_runs * back:]
            s_ref[idx, :, 0:n_keys] = lax.dot_general(
                q, k, (((1,), (1,)), ((), ())), preferred_element_type=F32) + bias

    def finish(branch, blocks, slot):
        n_runs, width, back = BRANCHES[branch]
        s_ref, p_ref = stage_refs[branch]
        plan = layout(branch, blocks, slot)
        for idx, starts, with_prev, n_keys in plan:
            s = s_ref[idx, :, 0:n_keys]
            m = jnp.max(s, axis=-1, keepdims=True)
            p_ref[idx, :, 0:n_keys] = jnp.exp(s - m).astype(BF16)
            for part, st in enumerate(starts):
                m_ref[branch, st:st + width, :] = jnp.broadcast_to(
                    m[part * width:(part + 1) * width], (width, ATT_HEAD_DIM))
        for idx, starts, with_prev, n_keys in plan:
            v = keys_of(v_ref, branch, starts, with_prev)
            v1 = jnp.concatenate([v, jnp.ones((n_keys, ATT_HEAD_DIM), BF16)], axis=1)
            pv = jnp.dot(p_ref[idx, :, 0:n_keys], v1, preferred_element_type=F32)
            for part, st in enumerate(starts):
                rows = slice(part * width, (part + 1) * width)
                num_ref[branch, st:st + width, :] = pv[rows, 0:ATT_HEAD_DIM]
                l_ref[branch, st:st + width, :] = pv[rows, ATT_HEAD_DIM:]

    groups = []
    for branch, (n_runs, width, back) in enumerate(BRANCHES):
        blocks = [(cls, i0, i0 > 0) for cls in range(MAX_DILATION // n_runs) for i0 in range(0, run_len, width)]
        size = ATT_GROUP[branch]
        groups += [(branch, blocks[g:g + size]) for g in range(0, len(blocks), size)]
    issue(*groups[0], 0)
    for n, group in enumerate(groups):
        if n + 1 < len(groups):
            issue(*groups[n + 1], (n + 1) % 2)
        finish(*group, n % 2)

    def combine(r, carry):
        sl = pl.ds(pl.multiple_of(r * run_len, run_len), run_len)
        m0, m1, m2 = m_ref[0, sl, :], m_ref[1, sl, :], m_ref[2, sl, :]
        top = jnp.maximum(jnp.maximum(m0, m1), m2)
        w0, w1, w2 = jnp.exp(m0 - top), jnp.exp(m1 - top), jnp.exp(m2 - top)
        den = w0 * l_ref[0, sl, :] + w1 * l_ref[1, sl, :] + w2 * l_ref[2, sl, :]
        num = w0 * num_ref[0, sl, :] + w1 * num_ref[1, sl, :] + w2 * num_ref[2, sl, :]
        nat_ref[pl.ds(r, run_len, stride=MAX_DILATION), :] = num / den
        return carry
    lax.fori_loop(0, MAX_DILATION, combine, 0)

    rows = 256

    def emit(c, carry):
        sl = pl.ds(pl.multiple_of(c * rows, rows), rows)
        o_ref[sl, :] = nat_ref[sl, :].astype(BF16)
        return carry
    lax.fori_loop(0, (MAX_DILATION * run_len) // rows, emit, 0)


def _stage_scratch():
    out = []
    for branch in (0, 1):
        n_runs, width, back = BRANCHES[branch]
        shape = (2 * ATT_GROUP[branch], n_runs * width, n_runs * (width + back))
        out += [pltpu.VMEM(shape, F32), pltpu.VMEM(shape, BF16)]
    return out


def _attention(qkv, batch, seq):
    run_len = seq // MAX_DILATION
    qkv2 = qkv.reshape(batch, seq, 3 * D_ATT)
    biases = [jnp.asarray(_branch_bias(*b)) for b in BRANCHES]

    def head_block(offset):
        return pl.BlockSpec((None, seq, ATT_HEAD_DIM), lambda b, h: (b, 0, offset + h))

    def full(shape):
        return pl.BlockSpec(shape, lambda b, h: (0,) * len(shape))

    return pl.pallas_call(
        functools.partial(_attn_kernel, run_len=run_len),
        grid=(batch, ATT_HEADS),
        in_specs=[head_block(0), head_block(ATT_HEADS), head_block(2 * ATT_HEADS)]
        + [full(b.shape) for b in biases],
        out_specs=pl.BlockSpec((None, seq, ATT_HEAD_DIM), lambda b, h: (b, 0, h)),
        out_shape=jax.ShapeDtypeStruct((batch, seq, D_ATT), BF16),
        scratch_shapes=[
            pltpu.VMEM((len(BRANCHES), seq, ATT_HEAD_DIM), F32),
            pltpu.VMEM((len(BRANCHES), seq, ATT_HEAD_DIM), F32),
            pltpu.VMEM((len(BRANCHES), seq, ATT_HEAD_DIM), F32),
            pltpu.VMEM((seq, ATT_HEAD_DIM), F32),
        ] + _stage_scratch(),
        compiler_params=_params(("arbitrary", "arbitrary")),
        name="dilated_attn",
    )(qkv2, qkv2, qkv2, *biases)


OUT_TM = 512


def _outproj_kernel(ys_ref, ya_ref, w_ref, x_ref, gpost_ref, gpre_ref, h_ref, u_ref, mix_ref):
    mix_ref[...] = jnp.dot(ys_ref[...], w_ref[0:D_SSM, :], preferred_element_type=F32)
    mix_ref[...] += jnp.dot(ya_ref[...], w_ref[D_SSM:D_MIX, :], preferred_element_type=F32)
    chunk = 128

    def body(c, carry):
        sl = pl.ds(pl.multiple_of(c * chunk, chunk), chunk)
        h = x_ref[sl, :] + _rms_scale(mix_ref[sl, :], gpost_ref[...])
        h_ref[sl, :] = h
        u_ref[sl, :] = _rms_scale(h, gpre_ref[...]).astype(BF16)
        return carry
    lax.fori_loop(0, OUT_TM // chunk, body, 0)


def _out_projection(y_ssm, y_att, w_out, x2, g_post, g_pre):
    tokens = x2.shape[0]

    def rows(width):
        return pl.BlockSpec((OUT_TM, width), lambda i: (i, 0))

    def whole(shape):
        return pl.BlockSpec(shape, lambda i: (0, 0), pipeline_mode=pl.Buffered(1))

    return pl.pallas_call(
        _outproj_kernel,
        grid=(tokens // OUT_TM,),
        in_specs=[rows(D_SSM), rows(D_ATT), whole((D_MIX, D_MODEL)), rows(D_MODEL),
                  whole((1, D_MODEL)), whole((1, D_MODEL))],
        out_specs=[rows(D_MODEL), rows(D_MODEL)],
        out_shape=[
            jax.ShapeDtypeStruct((tokens, D_MODEL), F32),
            jax.ShapeDtypeStruct((tokens, D_MODEL), BF16),
        ],
        scratch_shapes=[pltpu.VMEM((OUT_TM, D_MODEL), F32)],
        compiler_params=_params(("arbitrary",)),
        name="outproj",
    )(y_ssm, y_att, w_out, x2, g_post, g_pre)


MLP_TM = 512
MLP_TF = 2048
MLP_PART = 512


def _mlp_kernel(u_ref, wup_ref, wdown_ref, h_ref, g_ref, o_ref):
    f = pl.program_id(1)

    @pl.when(f == 0)
    def _():
        o_ref[...] = jnp.zeros_like(o_ref)

    for part in range(MLP_TF // MLP_PART):
        cols = slice(part * MLP_PART, (part + 1) * MLP_PART)
        hid = jnp.maximum(jnp.dot(u_ref[...], wup_ref[:, cols], preferred_element_type=F32), 0.0)
        o_ref[...] += jnp.dot((hid * hid).astype(BF16), wdown_ref[cols, :], preferred_element_type=F32)

    @pl.when(f == pl.num_programs(1) - 1)
    def _():
        chunk = 128

        def body(c, carry):
            sl = pl.ds(pl.multiple_of(c * chunk, chunk), chunk)
            o_ref[sl, :] = h_ref[sl, :] + _rms_scale(o_ref[sl, :], g_ref[...])
            return carry
        lax.fori_loop(0, MLP_TM // chunk, body, 0)


def _mlp(u2, w_up, w_down, h1, g_post):
    tokens = u2.shape[0]
    return pl.pallas_call(
        _mlp_kernel,
        grid=(tokens // MLP_TM, D_FF // MLP_TF),
        in_specs=[
            pl.BlockSpec((MLP_TM, D_MODEL), lambda i, f: (i, 0)),
            pl.BlockSpec((D_MODEL, MLP_TF), lambda i, f: (0, f)),
            pl.BlockSpec((MLP_TF, D_MODEL), lambda i, f: (f, 0)),
            pl.BlockSpec((MLP_TM, D_MODEL), lambda i, f: (i, 0)),
            pl.BlockSpec((1, D_MODEL), lambda i, f: (0, 0)),
        ],
        out_specs=pl.BlockSpec((MLP_TM, D_MODEL), lambda i, f: (i, 0)),
        out_shape=jax.ShapeDtypeStruct((tokens, D_MODEL), F32),
        compiler_params=_params(("arbitrary", "arbitrary")),
        name="mlp",
    )(u2, w_up, w_down, h1, g_post)


def kernel(x, norm_mix_pre, w_in, conv_w, conv_b, dt_bias, a_log, d_skip, ssm_norm_w, w_out,
           norm_mix_post, norm_mlp_pre, w_up, w_down, norm_mlp_post):
    batch, seq, _ = x.shape
    depth = w_in.shape[0]
    h = x.reshape(batch * seq, D_MODEL)
    for layer in range(depth):
        w_main, w_dt = _prep_in_weights(w_in[layer])
        zxbc, dt_raw, qkv = _in_projection(
            h, norm_mix_pre[layer].reshape(1, D_MODEL), w_main, w_dt, batch, seq)
        y_ssm = _ssd(zxbc, dt_raw, conv_w[layer], conv_b[layer], dt_bias[layer], a_log[layer],
                     d_skip[layer], ssm_norm_w[layer], batch, seq)
        y_att = _attention(qkv, batch, seq).reshape(batch * seq, D_ATT)
        h1, u2 = _out_projection(
            y_ssm, y_att, w_out[layer].astype(BF16), h,
            norm_mix_post[layer].reshape(1, D_MODEL), norm_mlp_pre[layer].reshape(1, D_MODEL))
        h = _mlp(u2, w_up[layer].astype(BF16), w_down[layer].astype(BF16), h1,
                 norm_mlp_post[layer].reshape(1, D_MODEL))
    return h.reshape(batch, seq, D_MODEL)
```

```python
import functools

import numpy as np
import jax
import jax.numpy as jnp
from jax import lax
from jax.experimental import pallas as pl
from jax.experimental.pallas import tpu as pltpu

F32 = jnp.float32
BF16 = jnp.bfloat16

D_MODEL = 2048
SSM_HEAD_DIM = 64
SSM_HEADS = 32
SSM_GROUPS = 8
HEADS_PER_GROUP = 4
D_STATE = 128
D_SSM = SSM_HEADS * SSM_HEAD_DIM
D_BC = 2 * SSM_GROUPS * D_STATE
GROUP_WIDTH = HEADS_PER_GROUP * SSM_HEAD_DIM
CONV_WIDTH = 4
CHUNK = 128
ATT_HEADS = 16
ATT_HEAD_DIM = 128
D_ATT = ATT_HEADS * ATT_HEAD_DIM
D_MIX = D_SSM + D_ATT
D_FF = 4 * D_MODEL
WINDOW_REACH = 128
MAX_DILATION = 16
EPS = 1e-6
ATT_SCALE = ATT_HEAD_DIM ** -0.5
MASKED = -1e30

LANES = 128
VMEM_LIMIT = 56 * 1024 * 1024


def _params(semantics):
    return pltpu.CompilerParams(dimension_semantics=semantics, vmem_limit_bytes=VMEM_LIMIT)


def _silu(v):
    return v * (1.0 / (1.0 + jnp.exp(-v)))


def _split3(v):
    hi = v.astype(BF16)
    rest = v - hi.astype(F32)
    mid = rest.astype(BF16)
    lo = (rest - mid.astype(F32)).astype(BF16)
    return hi, mid, lo


def _rms_scale(v, gain):
    ms = jnp.mean(v * v, axis=-1, keepdims=True)
    return (v * lax.rsqrt(ms + EPS)) * gain


PREP_COLS = 512


def _prep_kernel(wt_ref, wdt_ref, main_ref, dt_ref):
    main_ref[...] = wt_ref[...].T.astype(BF16)

    @pl.when(pl.program_id(0) == 0)
    def _():
        padded = jnp.concatenate([wdt_ref[...], jnp.zeros((LANES - SSM_HEADS, D_MODEL), F32)], axis=0)
        dt_ref[...] = padded.T.astype(BF16)


def _prep_in_weights(w):
    wt = w.T
    ssm_cols = 2 * D_SSM + D_BC
    att0 = ssm_cols + SSM_HEADS
    main_cols = ssm_cols + 3 * D_ATT
    n_ssm = ssm_cols // PREP_COLS

    def feature_row(j):
        return pl.multiple_of(jnp.where(j < n_ssm, j * PREP_COLS, att0 + (j - n_ssm) * PREP_COLS), 8)

    return pl.pallas_call(
        _prep_kernel,
        grid=(main_cols // PREP_COLS,),
        in_specs=[
            pl.BlockSpec((pl.Element(PREP_COLS), pl.Element(D_MODEL)), lambda j: (feature_row(j), 0)),
            pl.BlockSpec((pl.Element(SSM_HEADS), pl.Element(D_MODEL)), lambda j: (ssm_cols, 0)),
        ],
        out_specs=[pl.BlockSpec((D_MODEL, PREP_COLS), lambda j: (0, j)),
                   pl.BlockSpec((D_MODEL, LANES), lambda j: (0, 0))],
        out_shape=[jax.ShapeDtypeStruct((D_MODEL, main_cols), BF16),
                   jax.ShapeDtypeStruct((D_MODEL, LANES), BF16)],
        compiler_params=_params(("arbitrary",)),
        name="inproj_weights",
    )(wt, wt)


IN_TM = 1024
IN_TN = 1024
IN_STAGE_SLABS = 4
IN_STAGE_PITCH = 24


def _inproj_kernel(x_ref, g_ref, w_ref, wdt_ref, zxbc_ref, dt_ref, qkv_ref, u_ref, uperm_ref, uf_ref,
                   *, n_ssm_tiles):
    j = pl.program_id(1)
    chunk = 128
    n_slabs = D_MODEL // LANES
    rows = IN_TM // MAX_DILATION

    @pl.when(j == 0)
    def _():
        def body(c, carry):
            sl = pl.ds(pl.multiple_of(c * chunk, chunk), chunk)
            u_ref[sl, :] = _rms_scale(x_ref[sl, :], g_ref[...]).astype(BF16)
            return carry
        lax.fori_loop(0, IN_TM // chunk, body, 0)
        dt_ref[...] = jnp.dot(u_ref[...], wdt_ref[...], preferred_element_type=F32)
        for s0 in range(0, n_slabs, IN_STAGE_SLABS):
            for s in range(IN_STAGE_SLABS):
                cols = slice((s0 + s) * LANES, (s0 + s + 1) * LANES)
                for grp in range(rows):
                    uf_ref[s, grp * IN_STAGE_PITCH:grp * IN_STAGE_PITCH + MAX_DILATION, :] = (
                        u_ref[grp * MAX_DILATION:(grp + 1) * MAX_DILATION, cols].astype(F32))
            for r in range(MAX_DILATION):
                for s in range(IN_STAGE_SLABS):
                    uperm_ref[r * rows:(r + 1) * rows, (s0 + s) * LANES:(s0 + s + 1) * LANES] = (
                        uf_ref[s, pl.ds(r, rows, stride=IN_STAGE_PITCH), :].astype(BF16))

    @pl.when(j < n_ssm_tiles)
    def _():
        zxbc_ref[...] = jnp.dot(u_ref[...], w_ref[...], preferred_element_type=F32)

    @pl.when(j >= n_ssm_tiles)
    def _():
        scale = jnp.where(j < n_ssm_tiles + D_ATT // IN_TN, ATT_SCALE, 1.0).astype(F32)
        res = jnp.dot(uperm_ref[...], w_ref[...], preferred_element_type=F32) * scale
        for r in range(MAX_DILATION):
            qkv_ref[r] = res[r * rows:(r + 1) * rows, :].astype(BF16)


def _in_projection(x2, gain, w_main, w_dt, batch, seq):
    tokens = x2.shape[0]
    n_m = tokens // IN_TM
    m_per_seq = seq // IN_TM
    ssm_cols = 2 * D_SSM + D_BC
    att_cols = 3 * D_ATT
    n_ssm_tiles = ssm_cols // IN_TN
    n_att_tiles = att_cols // IN_TN
    run_len = seq // MAX_DILATION
    return pl.pallas_call(
        functools.partial(_inproj_kernel, n_ssm_tiles=n_ssm_tiles),
        grid=(n_m, n_ssm_tiles + n_att_tiles),
        in_specs=[
            pl.BlockSpec((IN_TM, D_MODEL), lambda i, j: (i, 0)),
            pl.BlockSpec((1, D_MODEL), lambda i, j: (0, 0)),
            pl.BlockSpec((D_MODEL, IN_TN), lambda i, j: (0, j)),
            pl.BlockSpec((D_MODEL, LANES), lambda i, j: (0, 0)),
        ],
        out_specs=[
            pl.BlockSpec((IN_TM, IN_TN), lambda i, j: (i, jnp.minimum(j, n_ssm_tiles - 1))),
            pl.BlockSpec((IN_TM, LANES), lambda i, j: (i, 0)),
            pl.BlockSpec((None, MAX_DILATION, IN_TM // MAX_DILATION, IN_TN),
                         lambda i, j: (i // m_per_seq, 0, i % m_per_seq, jnp.maximum(j - n_ssm_tiles, 0))),
        ],
        out_shape=[
            jax.ShapeDtypeStruct((tokens, ssm_cols), F32),
            jax.ShapeDtypeStruct((tokens, LANES), F32),
            jax.ShapeDtypeStruct((batch, MAX_DILATION, run_len, att_cols), BF16),
        ],
        scratch_shapes=[
            pltpu.VMEM((IN_TM, D_MODEL), BF16),
            pltpu.VMEM((IN_TM, D_MODEL), BF16),
            pltpu.VMEM((IN_STAGE_SLABS, IN_TM // MAX_DILATION * IN_STAGE_PITCH, LANES), F32),
        ],
        compiler_params=_params(("arbitrary", "arbitrary")),
        name="inproj",
    )(x2, gain, w_main, w_dt)


CONV_PAD = 8
SSD_ROWS = 2 * CHUNK


def _ssd_kernel(z_ref, x_ref, bc_ref, dt_ref, cwx_ref, cbx_ref, cwbc_ref, cbbc_ref,
                dtb_ref, alog_ref, dsk_ref, nw_ref, tri_ref, expand_ref,
                y_ref, state_ref, xpad_ref, bcpad_ref):
    @pl.when(pl.program_id(1) == 0)
    def _():
        state_ref[...] = jnp.zeros_like(state_ref)
        xpad_ref[0:CONV_PAD, :] = jnp.zeros((CONV_PAD, D_SSM), F32)
        bcpad_ref[0:CONV_PAD, :] = jnp.zeros((CONV_PAD, D_BC), F32)

    xpad_ref[CONV_PAD:CONV_PAD + SSD_ROWS, :] = x_ref[...]
    bcpad_ref[CONV_PAD:CONV_PAD + SSD_ROWS, :] = bc_ref[...]

    def conv_silu(pad_ref, w_ref, b_ref, r0, c0, width):
        cols = slice(c0, c0 + width)
        acc = b_ref[:, cols]
        for k in range(CONV_WIDTH):
            start = r0 + CONV_PAD - (CONV_WIDTH - 1) + k
            acc = acc + w_ref[k:k + 1, cols] * pad_ref[start:start + CHUNK, cols]
        return _silu(acc)

    tri = tri_ref[...]
    expand = expand_ref[...]
    row_i = lax.broadcasted_iota(jnp.int32, (CHUNK, CHUNK), 0)
    col_j = lax.broadcasted_iota(jnp.int32, (CHUNK, CHUNK), 1)
    causal = row_i >= col_j
    lane_head = lax.broadcasted_iota(jnp.int32, (CHUNK, GROUP_WIDTH), 1) // SSM_HEAD_DIM
    own_lanes = [jnp.where(lane_head == r, 1.0, 0.0).astype(BF16) for r in range(HEADS_PER_GROUP)]

    for r0 in range(0, SSD_ROWS, CHUNK):
        rows = slice(r0, r0 + CHUNK)
        dt_in = dt_ref[rows, :] + dtb_ref[...]
        dtv = jnp.maximum(dt_in, 0.0) + jnp.log1p(jnp.exp(-jnp.abs(dt_in)))
        da = dtv * (-jnp.exp(alog_ref[...]))
        a_cs = sum(jnp.dot(tri, part, preferred_element_type=F32)
                   for part in _split3(da))
        a_last = a_cs[CHUNK - 1:CHUNK, :]
        w_end = dtv * jnp.exp(a_last - a_cs)
        a_cs_t = a_cs.T
        dt_t = dtv.T
        w_end_t = w_end.T
        chunk_decay = sum(jnp.dot(part, expand, preferred_element_type=F32)
                          for part in _split3(jnp.broadcast_to(jnp.exp(a_last), (8, LANES)))
                          )[0:1, :]

        for g in range(SSM_GROUPS):
            gcols = slice(g * GROUP_WIDTH, (g + 1) * GROUP_WIDTH)
            xs = conv_silu(xpad_ref, cwx_ref, cbx_ref, r0, g * GROUP_WIDTH, GROUP_WIDTH)
            bm = conv_silu(bcpad_ref, cwbc_ref, cbbc_ref, r0, g * D_STATE, D_STATE)
            cm = conv_silu(bcpad_ref, cwbc_ref, cbbc_ref, r0, D_BC // 2 + g * D_STATE, D_STATE)
            cb = lax.dot_general(cm.astype(BF16), bm.astype(BF16), (((1,), (1,)), ((), ())),
                                 preferred_element_type=F32)
            bm_t = bm.T
            prev = state_ref[g]
            xs16 = xs.astype(BF16)
            prev16 = prev.astype(BF16)

            lhs_y, rhs_y, lhs_s, rhs_s = [], [], [], []
            for r in range(HEADS_PER_GROUP):
                h = g * HEADS_PER_GROUP + r
                col_a = jnp.broadcast_to(a_cs[:, h:h + 1], (CHUNK, CHUNK))
                row_a = a_cs_t[h:h + 1, :]
                decay = jnp.exp(jnp.where(causal, col_a - row_a, -jnp.inf))
                lhs_y.append((cb * decay * dt_t[h:h + 1, :]).astype(BF16))
                lhs_y.append((cm * jnp.exp(col_a)).astype(BF16))
                x_own = xs16 * own_lanes[r]
                rhs_y.append(x_own)
                rhs_y.append(prev16 * own_lanes[r])
                lhs_s.append((bm_t * w_end_t[h:h + 1, :]).astype(BF16))
                rhs_s.append(x_own)

            y = jnp.dot(jnp.concatenate(lhs_y, axis=1), jnp.concatenate(rhs_y, axis=0),
                        preferred_element_type=F32)
            s_new = jnp.dot(jnp.concatenate(lhs_s, axis=1), jnp.concatenate(rhs_s, axis=0),
                            preferred_element_type=F32)
            state_ref[g] = prev * chunk_decay[:, gcols] + s_new

            y = y + dsk_ref[:, gcols] * xs
            gated = y * _silu(z_ref[rows, gcols])
            y_ref[rows, gcols] = _rms_scale(gated, nw_ref[:, gcols]).astype(BF16)

    xpad_ref[0:CONV_PAD, :] = x_ref[SSD_ROWS - CONV_PAD:SSD_ROWS, :]
    bcpad_ref[0:CONV_PAD, :] = bc_ref[SSD_ROWS - CONV_PAD:SSD_ROWS, :]


def _ssd(zxbc, dt_raw, conv_w, conv_b, dt_bias, a_log, d_skip, ssm_norm_w, batch, seq):
    tokens = zxbc.shape[0]
    n_steps = seq // SSD_ROWS
    pad_heads = LANES - SSM_HEADS
    dtb = jnp.pad(dt_bias.astype(F32), (0, pad_heads)).reshape(1, LANES)
    alog = jnp.pad(a_log.astype(F32), (0, pad_heads)).reshape(1, LANES)
    dsk = jnp.repeat(d_skip.astype(F32), SSM_HEAD_DIM).reshape(1, D_SSM)
    tri = jnp.asarray(np.tril(np.ones((CHUNK, CHUNK), np.float32)), dtype=BF16)
    expand = np.zeros((LANES, D_SSM), np.float32)
    for h in range(SSM_HEADS):
        expand[h, h * SSM_HEAD_DIM:(h + 1) * SSM_HEAD_DIM] = 1.0
    expand = jnp.asarray(expand, dtype=BF16)

    def rows(b, c):
        return b * n_steps + c

    def full(shape):
        return pl.BlockSpec(shape, lambda b, c: (0,) * len(shape))

    return pl.pallas_call(
        _ssd_kernel,
        grid=(batch, n_steps),
        in_specs=[
            pl.BlockSpec((SSD_ROWS, D_SSM), lambda b, c: (rows(b, c), 0)),
            pl.BlockSpec((SSD_ROWS, D_SSM), lambda b, c: (rows(b, c), 1)),
            pl.BlockSpec((SSD_ROWS, D_BC), lambda b, c: (rows(b, c), 2)),
            pl.BlockSpec((SSD_ROWS, LANES), lambda b, c: (rows(b, c), 0)),
            full((CONV_WIDTH, D_SSM)), full((1, D_SSM)),
            full((CONV_WIDTH, D_BC)), full((1, D_BC)),
            full((1, LANES)), full((1, LANES)), full((1, D_SSM)), full((1, D_SSM)),
            full((CHUNK, CHUNK)), full((LANES, D_SSM)),
        ],
        out_specs=pl.BlockSpec((SSD_ROWS, D_SSM), lambda b, c: (rows(b, c), 0)),
        out_shape=jax.ShapeDtypeStruct((tokens, D_SSM), BF16),
        scratch_shapes=[
            pltpu.VMEM((SSM_GROUPS, D_STATE, GROUP_WIDTH), F32),
            pltpu.VMEM((CONV_PAD + SSD_ROWS, D_SSM), F32),
            pltpu.VMEM((CONV_PAD + SSD_ROWS, D_BC), F32),
        ],
        compiler_params=_params(("arbitrary", "arbitrary")),
        name="ssd",
    )(zxbc, zxbc, zxbc, dt_raw,
      conv_w[:, :D_SSM], conv_b[:D_SSM].reshape(1, D_SSM),
      conv_w[:, D_SSM:], conv_b[D_SSM:].reshape(1, D_BC),
      dtb, alog, dsk, ssm_norm_w.reshape(1, D_SSM), tri, expand)


BRANCHES = ((16, 16, 16), (4, 32, 32), (1, 128, 128))
ATT_GROUP = (3, 6, 6)
assert ATT_GROUP[1] == ATT_GROUP[2]


def _branch_bias(n_runs, width, back):
    s_q = np.repeat(np.arange(n_runs), width)
    i_q = np.tile(np.arange(width), n_runs)
    j_q = n_runs * i_q + s_q
    s_p = np.repeat(np.arange(n_runs), back)
    i_p = np.tile(np.arange(back), n_runs) - back
    j_k = np.concatenate([n_runs * i_p + s_p, j_q])
    dist = j_q[:, None] - j_k[None, :]
    return np.where((dist >= 0) & (dist <= WINDOW_REACH), 0.0, MASKED).astype(np.float32)


def _attn_kernel(q_ref, k_ref, v_ref, b0_ref, b1_ref, b2_ref, o_ref,
                 num_ref, m_ref, l_ref, nat_ref, s0_ref, p0_ref, s1_ref, p1_ref, *, run_len):
    bias_refs = (b0_ref, b1_ref, b2_ref)
    stage_refs = ((s0_ref, p0_ref), (s1_ref, p1_ref), (s1_ref, p1_ref))

    def rows_of(ref, starts, size):
        parts = [ref[st:st + size, :] for st in starts]
        return parts[0] if len(parts) == 1 else jnp.concatenate(parts, axis=0)

    def keys_of(ref, branch, starts, with_prev):
        n_runs, width, back = BRANCHES[branch]
        cur = rows_of(ref, starts, width)
        if not with_prev:
            return cur
        return jnp.concatenate([rows_of(ref, [st - back for st in starts], back), cur], axis=0)

    def layout(branch, blocks, slot):
        n_runs, width, back = BRANCHES[branch]
        n_cls = MAX_DILATION // n_runs
        out = []
        for u, (cls, i0, with_prev) in enumerate(blocks):
            starts = [(cls + n_cls * s) * run_len + i0 for s in range(n_runs)]
            out.append((slot * ATT_GROUP[branch] + u, starts, with_prev,
                        n_runs * (width + (back if with_prev else 0))))
        return out

    def issue(branch, blocks, slot):
        n_runs, width, back = BRANCHES[branch]
        s_ref, _ = stage_refs[branch]
        for idx, starts, with_prev, n_keys in layout(branch, blocks, slot):
            q = rows_of(q_ref, starts, width)
            k = keys_of(k_ref, branch, starts, with_prev)
            bias = bias_refs[branch][...] if with_prev else bias_refs[branch][:, n_runs * back:]
            s_ref[idx, :, 0:n_keys] = lax.dot_general(
                q, k, (((1,), (1,)), ((), ())), preferred_element_type=F32) + bias

    def finish(branch, blocks, slot):
        n_runs, width, back = BRANCHES[branch]
        s_ref, p_ref = stage_refs[branch]
        plan = layout(branch, blocks, slot)
        for idx, starts, with_prev, n_keys in plan:
            s = s_ref[idx, :, 0:n_keys]
            m = jnp.max(s, axis=-1, keepdims=True)
            p_ref[idx, :, 0:n_keys] = jnp.exp(s - m).astype(BF16)
            for part, st in enumerate(starts):
                m_ref[branch, st:st + width, :] = jnp.broadcast_to(
                    m[part * width:(part + 1) * width], (width, ATT_HEAD_DIM))
        for idx, starts, with_prev, n_keys in plan:
            v = keys_of(v_ref, branch, starts, with_prev)
            v1 = jnp.concatenate([v, jnp.ones((n_keys, ATT_HEAD_DIM), BF16)], axis=1)
            pv = jnp.dot(p_ref[idx, :, 0:n_keys], v1, preferred_element_type=F32)
            for part, st in enumerate(starts):
                rows = slice(part * width, (part + 1) * width)
                num_ref[branch, st:st + width, :] = pv[rows, 0:ATT_HEAD_DIM]
                l_ref[branch, st:st + width, :] = pv[rows, ATT_HEAD_DIM:]

    groups = []
    for branch, (n_runs, width, back) in enumerate(BRANCHES):
        blocks = [(cls, i0, i0 > 0) for cls in range(MAX_DILATION // n_runs) for i0 in range(0, run_len, width)]
        size = ATT_GROUP[branch]
        groups += [(branch, blocks[g:g + size]) for g in range(0, len(blocks), size)]
    issue(*groups[0], 0)
    for n, group in enumerate(groups):
        if n + 1 < len(groups):
            issue(*groups[n + 1], (n + 1) % 2)
        finish(*group, n % 2)

    def combine(r, carry):
        sl = pl.ds(pl.multiple_of(r * run_len, run_len), run_len)
        m0, m1, m2 = m_ref[0, sl, :], m_ref[1, sl, :], m_ref[2, sl, :]
        top = jnp.maximum(jnp.maximum(m0, m1), m2)
        w0, w1, w2 = jnp.exp(m0 - top), jnp.exp(m1 - top), jnp.exp(m2 - top)
        den = w0 * l_ref[0, sl, :] + w1 * l_ref[1, sl, :] + w2 * l_ref[2, sl, :]
        num = w0 * num_ref[0, sl, :] + w1 * num_ref[1, sl, :] + w2 * num_ref[2, sl, :]
        nat_ref[pl.ds(r, run_len, stride=MAX_DILATION), :] = num / den
        return carry
    lax.fori_loop(0, MAX_DILATION, combine, 0)

    rows = 256

    def emit(c, carry):
        sl = pl.ds(pl.multiple_of(c * rows, rows), rows)
        o_ref[sl, :] = nat_ref[sl, :].astype(BF16)
        return carry
    lax.fori_loop(0, (MAX_DILATION * run_len) // rows, emit, 0)


def _stage_scratch():
    out = []
    for branch in (0, 1):
        n_runs, width, back = BRANCHES[branch]
        shape = (2 * ATT_GROUP[branch], n_runs * width, n_runs * (width + back))
        out += [pltpu.VMEM(shape, F32), pltpu.VMEM(shape, BF16)]
    return out


def _attention(qkv, batch, seq):
    run_len = seq // MAX_DILATION
    qkv2 = qkv.reshape(batch, seq, 3 * D_ATT)
    biases = [jnp.asarray(_branch_bias(*b)) for b in BRANCHES]

    def head_block(offset):
        return pl.BlockSpec((None, seq, ATT_HEAD_DIM), lambda b, h: (b, 0, offset + h))

    def full(shape):
        return pl.BlockSpec(shape, lambda b, h: (0,) * len(shape))

    return pl.pallas_call(
        functools.partial(_attn_kernel, run_len=run_len),
        grid=(batch, ATT_HEADS),
        in_specs=[head_block(0), head_block(ATT_HEADS), head_block(2 * ATT_HEADS)]
        + [full(b.shape) for b in biases],
        out_specs=pl.BlockSpec((None, seq, ATT_HEAD_DIM), lambda b, h: (b, 0, h)),
        out_shape=jax.ShapeDtypeStruct((batch, seq, D_ATT), BF16),
        scratch_shapes=[
            pltpu.VMEM((len(BRANCHES), seq, ATT_HEAD_DIM), F32),
            pltpu.VMEM((len(BRANCHES), seq, ATT_HEAD_DIM), F32),
            pltpu.VMEM((len(BRANCHES), seq, ATT_HEAD_DIM), F32),
            pltpu.VMEM((seq, ATT_HEAD_DIM), F32),
        ] + _stage_scratch(),
        compiler_params=_params(("arbitrary", "arbitrary")),
        name="dilated_attn",
    )(qkv2, qkv2, qkv2, *biases)


OUT_TM = 512


def _outproj_kernel(ys_ref, ya_ref, w_ref, x_ref, gpost_ref, gpre_ref, h_ref, u_ref, mix_ref):
    mix_ref[...] = jnp.dot(ys_ref[...], w_ref[0:D_SSM, :], preferred_element_type=F32)
    mix_ref[...] += jnp.dot(ya_ref[...], w_ref[D_SSM:D_MIX, :], preferred_element_type=F32)
    chunk = 128

    def body(c, carry):
        sl = pl.ds(pl.multiple_of(c * chunk, chunk), chunk)
        h = x_ref[sl, :] + _rms_scale(mix_ref[sl, :], gpost_ref[...])
        h_ref[sl, :] = h
        u_ref[sl, :] = _rms_scale(h, gpre_ref[...]).astype(BF16)
        return carry
    lax.fori_loop(0, OUT_TM // chunk, body, 0)


def _out_projection(y_ssm, y_att, w_out, x2, g_post, g_pre):
    tokens = x2.shape[0]

    def rows(width):
        return pl.BlockSpec((OUT_TM, width), lambda i: (i, 0))

    def whole(shape):
        return pl.BlockSpec(shape, lambda i: (0, 0), pipeline_mode=pl.Buffered(1))

    return pl.pallas_call(
        _outproj_kernel,
        grid=(tokens // OUT_TM,),
        in_specs=[rows(D_SSM), rows(D_ATT), whole((D_MIX, D_MODEL)), rows(D_MODEL),
                  whole((1, D_MODEL)), whole((1, D_MODEL))],
        out_specs=[rows(D_MODEL), rows(D_MODEL)],
        out_shape=[
            jax.ShapeDtypeStruct((tokens, D_MODEL), F32),
            jax.ShapeDtypeStruct((tokens, D_MODEL), BF16),
        ],
        scratch_shapes=[pltpu.VMEM((OUT_TM, D_MODEL), F32)],
        compiler_params=_params(("arbitrary",)),
        name="outproj",
    )(y_ssm, y_att, w_out, x2, g_post, g_pre)


MLP_TM = 512
MLP_TF = 2048
MLP_PART = 512


def _mlp_kernel(u_ref, wup_ref, wdown_ref, h_ref, g_ref, o_ref):
    f = pl.program_id(1)

    @pl.when(f == 0)
    def _():
        o_ref[...] = jnp.zeros_like(o_ref)

    for part in range(MLP_TF // MLP_PART):
        cols = slice(part * MLP_PART, (part + 1) * MLP_PART)
        hid = jnp.maximum(jnp.dot(u_ref[...], wup_ref[:, cols], preferred_element_type=F32), 0.0)
        o_ref[...] += jnp.dot((hid * hid).astype(BF16), wdown_ref[cols, :], preferred_element_type=F32)

    @pl.when(f == pl.num_programs(1) - 1)
    def _():
        chunk = 128

        def body(c, carry):
            sl = pl.ds(pl.multiple_of(c * chunk, chunk), chunk)
            o_ref[sl, :] = h_ref[sl, :] + _rms_scale(o_ref[sl, :], g_ref[...])
            return carry
        lax.fori_loop(0, MLP_TM // chunk, body, 0)


def _mlp(u2, w_up, w_down, h1, g_post):
    tokens = u2.shape[0]
    return pl.pallas_call(
        _mlp_kernel,
        grid=(tokens // MLP_TM, D_FF // MLP_TF),
        in_specs=[
            pl.BlockSpec((MLP_TM, D_MODEL), lambda i, f: (i, 0)),
            pl.BlockSpec((D_MODEL, MLP_TF), lambda i, f: (0, f)),
            pl.BlockSpec((MLP_TF, D_MODEL), lambda i, f: (f, 0)),
            pl.BlockSpec((MLP_TM, D_MODEL), lambda i, f: (i, 0)),
            pl.BlockSpec((1, D_MODEL), lambda i, f: (0, 0)),
        ],
        out_specs=pl.BlockSpec((MLP_TM, D_MODEL), lambda i, f: (i, 0)),
        out_shape=jax.ShapeDtypeStruct((tokens, D_MODEL), F32),
        compiler_params=_params(("arbitrary", "arbitrary")),
        name="mlp",
    )(u2, w_up, w_down, h1, g_post)


def kernel(x, norm_mix_pre, w_in, conv_w, conv_b, dt_bias, a_log, d_skip, ssm_norm_w, w_out,
           norm_mix_post, norm_mlp_pre, w_up, w_down, norm_mlp_post):
    batch, seq, _ = x.shape
    depth = w_in.shape[0]
    h = x.reshape(batch * seq, D_MODEL)
    for layer in range(depth):
        w_main, w_dt = _prep_in_weights(w_in[layer])
        zxbc, dt_raw, qkv = _in_projection(
            h, norm_mix_pre[layer].reshape(1, D_MODEL), w_main, w_dt, batch, seq)
        y_ssm = _ssd(zxbc, dt_raw, conv_w[layer], conv_b[layer], dt_bias[layer], a_log[layer],
                     d_skip[layer], ssm_norm_w[layer], batch, seq)
        y_att = _attention(qkv, batch, seq).reshape(batch * seq, D_ATT)
        h1, u2 = _out_projection(
            y_ssm, y_att, w_out[layer].astype(BF16), h,
            norm_mix_post[layer].reshape(1, D_MODEL), norm_mlp_pre[layer].reshape(1, D_MODEL))
        h = _mlp(u2, w_up[layer].astype(BF16), w_down[layer].astype(BF16), h1,
                 norm_mlp_post[layer].reshape(1, D_MODEL))
    return h.reshape(batch, seq, D_MODEL)
```

```python
import functools

import numpy as np
import jax
import jax.numpy as jnp
from jax import lax
from jax.experimental import pallas as pl
from jax.experimental.pallas import tpu as pltpu

F32 = jnp.float32
BF16 = jnp.bfloat16

D_MODEL = 2048
SSM_HEAD_DIM = 64
SSM_HEADS = 32
SSM_GROUPS = 8
HEADS_PER_GROUP = 4
D_STATE = 128
D_SSM = SSM_HEADS * SSM_HEAD_DIM
D_BC = 2 * SSM_GROUPS * D_STATE
GROUP_WIDTH = HEADS_PER_GROUP * SSM_HEAD_DIM
CONV_WIDTH = 4
CHUNK = 128
ATT_HEADS = 16
ATT_HEAD_DIM = 128
D_ATT = ATT_HEADS * ATT_HEAD_DIM
D_MIX = D_SSM + D_ATT
D_FF = 4 * D_MODEL
WINDOW_REACH = 128
MAX_DILATION = 16
EPS = 1e-6
ATT_SCALE = ATT_HEAD_DIM ** -0.5
MASKED = -1e30

LANES = 128
VMEM_LIMIT = 56 * 1024 * 1024


def _params(semantics):
    return pltpu.CompilerParams(dimension_semantics=semantics, vmem_limit_bytes=VMEM_LIMIT)


def _silu(v):
    half = 0.5 * v
    return half + half * jnp.tanh(half)


def _split3(v):
    hi = v.astype(BF16)
    rest = v - hi.astype(F32)
    mid = rest.astype(BF16)
    lo = (rest - mid.astype(F32)).astype(BF16)
    return hi, mid, lo


def _rms_scale(v, gain):
    ms = jnp.mean(v * v, axis=-1, keepdims=True)
    return (v * lax.rsqrt(ms + EPS)) * gain


PREP_COLS = 512


def _prep_kernel(wt_ref, wdt_ref, main_ref, dt_ref):
    main_ref[...] = wt_ref[...].T.astype(BF16)

    @pl.when(pl.program_id(0) == 0)
    def _():
        padded = jnp.concatenate([wdt_ref[...], jnp.zeros((LANES - SSM_HEADS, D_MODEL), F32)], axis=0)
        dt_ref[...] = padded.T.astype(BF16)


def _prep_in_weights(w):
    wt = w.T
    ssm_cols = 2 * D_SSM + D_BC
    att0 = ssm_cols + SSM_HEADS
    main_cols = ssm_cols + 3 * D_ATT
    n_ssm = ssm_cols // PREP_COLS

    def feature_row(j):
        return pl.multiple_of(jnp.where(j < n_ssm, j * PREP_COLS, att0 + (j - n_ssm) * PREP_COLS), 8)

    return pl.pallas_call(
        _prep_kernel,
        grid=(main_cols // PREP_COLS,),
        in_specs=[
            pl.BlockSpec((pl.Element(PREP_COLS), pl.Element(D_MODEL)), lambda j: (feature_row(j), 0)),
            pl.BlockSpec((pl.Element(SSM_HEADS), pl.Element(D_MODEL)), lambda j: (ssm_cols, 0)),
        ],
        out_specs=[pl.BlockSpec((D_MODEL, PREP_COLS), lambda j: (0, j)),
                   pl.BlockSpec((D_MODEL, LANES), lambda j: (0, 0))],
        out_shape=[jax.ShapeDtypeStruct((D_MODEL, main_cols), BF16),
                   jax.ShapeDtypeStruct((D_MODEL, LANES), BF16)],
        compiler_params=_params(("arbitrary",)),
        name="inproj_weights",
    )(wt, wt)


IN_TM = 1024
IN_TN = 1024
IN_STAGE_SLABS = 4
IN_STAGE_PITCH = 24


def _inproj_kernel(x_ref, g_ref, w_ref, wdt_ref, zxbc_ref, dt_ref, qkv_ref, u_ref, uperm_ref, uf_ref,
                   *, n_ssm_tiles):
    j = pl.program_id(1)
    chunk = 128
    n_slabs = D_MODEL // LANES
    rows = IN_TM // MAX_DILATION

    @pl.when(j == 0)
    def _():
        def body(c, carry):
            sl = pl.ds(pl.multiple_of(c * chunk, chunk), chunk)
            u_ref[sl, :] = _rms_scale(x_ref[sl, :], g_ref[...]).astype(BF16)
            return carry
        lax.fori_loop(0, IN_TM // chunk, body, 0)
        dt_ref[...] = jnp.dot(u_ref[...], wdt_ref[...], preferred_element_type=F32)
        for s0 in range(0, n_slabs, IN_STAGE_SLABS):
            for s in range(IN_STAGE_SLABS):
                cols = slice((s0 + s) * LANES, (s0 + s + 1) * LANES)
                for grp in range(rows):
                    uf_ref[s, grp * IN_STAGE_PITCH:grp * IN_STAGE_PITCH + MAX_DILATION, :] = (
                        u_ref[grp * MAX_DILATION:(grp + 1) * MAX_DILATION, cols].astype(F32))
            for r in range(MAX_DILATION):
                for s in range(IN_STAGE_SLABS):
                    uperm_ref[r * rows:(r + 1) * rows, (s0 + s) * LANES:(s0 + s + 1) * LANES] = (
                        uf_ref[s, pl.ds(r, rows, stride=IN_STAGE_PITCH), :].astype(BF16))

    @pl.when(j < n_ssm_tiles)
    def _():
        zxbc_ref[...] = jnp.dot(u_ref[...], w_ref[...], preferred_element_type=F32)

    @pl.when(j >= n_ssm_tiles)
    def _():
        scale = jnp.where(j < n_ssm_tiles + D_ATT // IN_TN, ATT_SCALE, 1.0).astype(F32)
        res = jnp.dot(uperm_ref[...], w_ref[...], preferred_element_type=F32) * scale
        for r in range(MAX_DILATION):
            qkv_ref[r] = res[r * rows:(r + 1) * rows, :].astype(BF16)


def _in_projection(x2, gain, w_main, w_dt, batch, seq):
    tokens = x2.shape[0]
    n_m = tokens // IN_TM
    m_per_seq = seq // IN_TM
    ssm_cols = 2 * D_SSM + D_BC
    att_cols = 3 * D_ATT
    n_ssm_tiles = ssm_cols // IN_TN
    n_att_tiles = att_cols // IN_TN
    run_len = seq // MAX_DILATION
    return pl.pallas_call(
        functools.partial(_inproj_kernel, n_ssm_tiles=n_ssm_tiles),
        grid=(n_m, n_ssm_tiles + n_att_tiles),
        in_specs=[
            pl.BlockSpec((IN_TM, D_MODEL), lambda i, j: (i, 0)),
            pl.BlockSpec((1, D_MODEL), lambda i, j: (0, 0)),
            pl.BlockSpec((D_MODEL, IN_TN), lambda i, j: (0, j)),
            pl.BlockSpec((D_MODEL, LANES), lambda i, j: (0, 0)),
        ],
        out_specs=[
            pl.BlockSpec((IN_TM, IN_TN), lambda i, j: (i, jnp.minimum(j, n_ssm_tiles - 1))),
            pl.BlockSpec((IN_TM, LANES), lambda i, j: (i, 0)),
            pl.BlockSpec((None, MAX_DILATION, IN_TM // MAX_DILATION, IN_TN),
                         lambda i, j: (i // m_per_seq, 0, i % m_per_seq, jnp.maximum(j - n_ssm_tiles, 0))),
        ],
        out_shape=[
            jax.ShapeDtypeStruct((tokens, ssm_cols), F32),
            jax.ShapeDtypeStruct((tokens, LANES), F32),
            jax.ShapeDtypeStruct((batch, MAX_DILATION, run_len, att_cols), BF16),
        ],
        scratch_shapes=[
            pltpu.VMEM((IN_TM, D_MODEL), BF16),
            pltpu.VMEM((IN_TM, D_MODEL), BF16),
            pltpu.VMEM((IN_STAGE_SLABS, IN_TM // MAX_DILATION * IN_STAGE_PITCH, LANES), F32),
        ],
        compiler_params=_params(("arbitrary", "arbitrary")),
        name="inproj",
    )(x2, gain, w_main, w_dt)


CONV_PAD = 8
SSD_ROWS = 2 * CHUNK


def _ssd_kernel(z_ref, x_ref, bc_ref, dt_ref, cwx_ref, cbx_ref, cwbc_ref, cbbc_ref,
                dtb_ref, alog_ref, dsk_ref, nw_ref, tri_ref, expand_ref,
                y_ref, state_ref, xpad_ref, bcpad_ref):
    @pl.when(pl.program_id(1) == 0)
    def _():
        state_ref[...] = jnp.zeros_like(state_ref)
        xpad_ref[0:CONV_PAD, :] = jnp.zeros((CONV_PAD, D_SSM), F32)
        bcpad_ref[0:CONV_PAD, :] = jnp.zeros((CONV_PAD, D_BC), F32)

    xpad_ref[CONV_PAD:CONV_PAD + SSD_ROWS, :] = x_ref[...]
    bcpad_ref[CONV_PAD:CONV_PAD + SSD_ROWS, :] = bc_ref[...]

    def conv_silu(pad_ref, w_ref, b_ref, r0, c0, width):
        cols = slice(c0, c0 + width)
        acc = b_ref[:, cols]
        for k in range(CONV_WIDTH):
            start = r0 + CONV_PAD - (CONV_WIDTH - 1) + k
            acc = acc + w_ref[k:k + 1, cols] * pad_ref[start:start + CHUNK, cols]
        return _silu(acc)

    tri = tri_ref[...]
    expand = expand_ref[...]
    row_i = lax.broadcasted_iota(jnp.int32, (CHUNK, CHUNK), 0)
    col_j = lax.broadcasted_iota(jnp.int32, (CHUNK, CHUNK), 1)
    causal = row_i >= col_j
    lane_head = lax.broadcasted_iota(jnp.int32, (CHUNK, GROUP_WIDTH), 1) // SSM_HEAD_DIM
    own_lanes = [jnp.where(lane_head == r, 1.0, 0.0).astype(BF16) for r in range(HEADS_PER_GROUP)]

    for r0 in range(0, SSD_ROWS, CHUNK):
        rows = slice(r0, r0 + CHUNK)
        dt_in = dt_ref[rows, :] + dtb_ref[...]
        dtv = jnp.maximum(dt_in, 0.0) + jnp.log1p(jnp.exp(-jnp.abs(dt_in)))
        da = dtv * (-jnp.exp(alog_ref[...]))
        a_cs = sum(jnp.dot(tri, part, preferred_element_type=F32)
                   for part in _split3(da))
        a_last = a_cs[CHUNK - 1:CHUNK, :]
        w_end = dtv * jnp.exp(a_last - a_cs)
        a_cs_t = a_cs.T
        dt_t = dtv.T
        w_end_t = w_end.T
        chunk_decay = sum(jnp.dot(part, expand, preferred_element_type=F32)
                          for part in _split3(jnp.broadcast_to(jnp.exp(a_last), (8, LANES)))
                          )[0:1, :]

        for g in range(SSM_GROUPS):
            gcols = slice(g * GROUP_WIDTH, (g + 1) * GROUP_WIDTH)
            xs = conv_silu(xpad_ref, cwx_ref, cbx_ref, r0, g * GROUP_WIDTH, GROUP_WIDTH)
            bm = conv_silu(bcpad_ref, cwbc_ref, cbbc_ref, r0, g * D_STATE, D_STATE)
            cm = conv_silu(bcpad_ref, cwbc_ref, cbbc_ref, r0, D_BC // 2 + g * D_STATE, D_STATE)
            cb = lax.dot_general(cm.astype(BF16), bm.astype(BF16), (((1,), (1,)), ((), ())),
                                 preferred_element_type=F32)
            bm_t = bm.T
            prev = state_ref[g]
            xs16 = xs.astype(BF16)
            prev16 = prev.astype(BF16)

            lhs_y, rhs_y, lhs_s, rhs_s = [], [], [], []
            for r in range(HEADS_PER_GROUP):
                h = g * HEADS_PER_GROUP + r
                col_a = jnp.broadcast_to(a_cs[:, h:h + 1], (CHUNK, CHUNK))
                row_a = a_cs_t[h:h + 1, :]
                decay = jnp.exp(jnp.where(causal, col_a - row_a, -jnp.inf))
                lhs_y.append((cb * decay * dt_t[h:h + 1, :]).astype(BF16))
                lhs_y.append((cm * jnp.exp(col_a)).astype(BF16))
                x_own = xs16 * own_lanes[r]
                rhs_y.append(x_own)
                rhs_y.append(prev16 * own_lanes[r])
                lhs_s.append((bm_t * w_end_t[h:h + 1, :]).astype(BF16))
                rhs_s.append(x_own)

            y = jnp.dot(jnp.concatenate(lhs_y, axis=1), jnp.concatenate(rhs_y, axis=0),
                        preferred_element_type=F32)
            s_new = jnp.dot(jnp.concatenate(lhs_s, axis=1), jnp.concatenate(rhs_s, axis=0),
                            preferred_element_type=F32)
            state_ref[g] = prev * chunk_decay[:, gcols] + s_new

            y = y + dsk_ref[:, gcols] * xs
            gated = y * _silu(z_ref[rows, gcols])
            y_ref[rows, gcols] = _rms_scale(gated, nw_ref[:, gcols]).astype(BF16)

    xpad_ref[0:CONV_PAD, :] = x_ref[SSD_ROWS - CONV_PAD:SSD_ROWS, :]
    bcpad_ref[0:CONV_PAD, :] = bc_ref[SSD_ROWS - CONV_PAD:SSD_ROWS, :]


def _ssd(zxbc, dt_raw, conv_w, conv_b, dt_bias, a_log, d_skip, ssm_norm_w, batch, seq):
    tokens = zxbc.shape[0]
    n_steps = seq // SSD_ROWS
    pad_heads = LANES - SSM_HEADS
    dtb = jnp.pad(dt_bias.astype(F32), (0, pad_heads)).reshape(1, LANES)
    alog = jnp.pad(a_log.astype(F32), (0, pad_heads)).reshape(1, LANES)
    dsk = jnp.repeat(d_skip.astype(F32), SSM_HEAD_DIM).reshape(1, D_SSM)
    tri = jnp.asarray(np.tril(np.ones((CHUNK, CHUNK), np.float32)), dtype=BF16)
    expand = np.zeros((LANES, D_SSM), np.float32)
    for h in range(SSM_HEADS):
        expand[h, h * SSM_HEAD_DIM:(h + 1) * SSM_HEAD_DIM] = 1.0
    expand = jnp.asarray(expand, dtype=BF16)

    def rows(b, c):
        return b * n_steps + c

    def full(shape):
        return pl.BlockSpec(shape, lambda b, c: (0,) * len(shape))

    return pl.pallas_call(
        _ssd_kernel,
        grid=(batch, n_steps),
        in_specs=[
            pl.BlockSpec((SSD_ROWS, D_SSM), lambda b, c: (rows(b, c), 0)),
            pl.BlockSpec((SSD_ROWS, D_SSM), lambda b, c: (rows(b, c), 1)),
            pl.BlockSpec((SSD_ROWS, D_BC), lambda b, c: (rows(b, c), 2)),
            pl.BlockSpec((SSD_ROWS, LANES), lambda b, c: (rows(b, c), 0)),
            full((CONV_WIDTH, D_SSM)), full((1, D_SSM)),
            full((CONV_WIDTH, D_BC)), full((1, D_BC)),
            full((1, LANES)), full((1, LANES)), full((1, D_SSM)), full((1, D_SSM)),
            full((CHUNK, CHUNK)), full((LANES, D_SSM)),
        ],
        out_specs=pl.BlockSpec((SSD_ROWS, D_SSM), lambda b, c: (rows(b, c), 0)),
        out_shape=jax.ShapeDtypeStruct((tokens, D_SSM), BF16),
        scratch_shapes=[
            pltpu.VMEM((SSM_GROUPS, D_STATE, GROUP_WIDTH), F32),
            pltpu.VMEM((CONV_PAD + SSD_ROWS, D_SSM), F32),
            pltpu.VMEM((CONV_PAD + SSD_ROWS, D_BC), F32),
        ],
        compiler_params=_params(("arbitrary", "arbitrary")),
        name="ssd",
    )(zxbc, zxbc, zxbc, dt_raw,
      conv_w[:, :D_SSM], conv_b[:D_SSM].reshape(1, D_SSM),
      conv_w[:, D_SSM:], conv_b[D_SSM:].reshape(1, D_BC),
      dtb, alog, dsk, ssm_norm_w.reshape(1, D_SSM), tri, expand)


BRANCHES = ((16, 16, 16), (4, 32, 32), (1, 128, 128))
ATT_GROUP = (3, 6, 6)
assert ATT_GROUP[1] == ATT_GROUP[2]


def _branch_bias(n_runs, width, back):
    s_q = np.repeat(np.arange(n_runs), width)
    i_q = np.tile(np.arange(width), n_runs)
    j_q = n_runs * i_q + s_q
    s_p = np.repeat(np.arange(n_runs), back)
    i_p = np.tile(np.arange(back), n_runs) - back
    j_k = np.concatenate([n_runs * i_p + s_p, j_q])
    dist = j_q[:, None] - j_k[None, :]
    return np.where((dist >= 0) & (dist <= WINDOW_REACH), 0.0, MASKED).astype(np.float32)


def _attn_kernel(q_ref, k_ref, v_ref, b0_ref, b1_ref, b2_ref, o_ref,
                 num_ref, m_ref, l_ref, nat_ref, s0_ref, p0_ref, s1_ref, p1_ref, *, run_len):
    bias_refs = (b0_ref, b1_ref, b2_ref)
    stage_refs = ((s0_ref, p0_ref), (s1_ref, p1_ref), (s1_ref, p1_ref))

    def rows_of(ref, starts, size):
        parts = [ref[st:st + size, :] for st in starts]
        return parts[0] if len(parts) == 1 else jnp.concatenate(parts, axis=0)

    def keys_of(ref, branch, starts, with_prev):
        n_runs, width, back = BRANCHES[branch]
        cur = rows_of(ref, starts, width)
        if not with_prev:
            return cur
        return jnp.concatenate([rows_of(ref, [st - back for st in starts], back), cur], axis=0)

    def layout(branch, blocks, slot):
        n_runs, width, back = BRANCHES[branch]
        n_cls = MAX_DILATION // n_runs
        out = []
        for u, (cls, i0, with_prev) in enumerate(blocks):
            starts = [(cls + n_cls * s) * run_len + i0 for s in range(n_runs)]
            out.append((slot * ATT_GROUP[branch] + u, starts, with_prev,
                        n_runs * (width + (back if with_prev else 0))))
        return out

    def issue(branch, blocks, slot):
        n_runs, width, back = BRANCHES[branch]
        s_ref, _ = stage_refs[branch]
        for idx, starts, with_prev, n_keys in layout(branch, blocks, slot):
            q = rows_of(q_ref, starts, width)
            k = keys_of(k_ref, branch, starts, with_prev)
            bias = bias_refs[branch][...] if with_prev else bias_refs[branch][:, n_runs * back:]
            s_ref[idx, :, 0:n_keys] = lax.dot_general(
                q, k, (((1,), (1,)), ((), ())), preferred_element_type=F32) + bias

    def finish(branch, blocks, slot):
        n_runs, width, back = BRANCHES[branch]
        s_ref, p_ref = stage_refs[branch]
        plan = layout(branch, blocks, slot)
        for idx, starts, with_prev, n_keys in plan:
            s = s_ref[idx, :, 0:n_keys]
            m = jnp.max(s, axis=-1, keepdims=True)
            p_ref[idx, :, 0:n_keys] = jnp.exp(s - m).astype(BF16)
            for part, st in enumerate(starts):
                m_ref[branch, st:st + width, :] = jnp.broadcast_to(
                    m[part * width:(part + 1) * width], (width, ATT_HEAD_DIM))
        for idx, starts, with_prev, n_keys in plan:
            v = keys_of(v_ref, branch, starts, with_prev)
            v1 = jnp.concatenate([v, jnp.ones((n_keys, ATT_HEAD_DIM), BF16)], axis=1)
            pv = jnp.dot(p_ref[idx, :, 0:n_keys], v1, preferred_element_type=F32)
            for part, st in enumerate(starts):
                rows = slice(part * width, (part + 1) * width)
                num_ref[branch, st:st + width, :] = pv[rows, 0:ATT_HEAD_DIM]
                l_ref[branch, st:st + width, :] = pv[rows, ATT_HEAD_DIM:]

    groups = []
    for branch, (n_runs, width, back) in enumerate(BRANCHES):
        blocks = [(cls, i0, i0 > 0) for cls in range(MAX_DILATION // n_runs) for i0 in range(0, run_len, width)]
        size = ATT_GROUP[branch]
        groups += [(branch, blocks[g:g + size]) for g in range(0, len(blocks), size)]
    issue(*groups[0], 0)
    for n, group in enumerate(groups):
        if n + 1 < len(groups):
            issue(*groups[n + 1], (n + 1) % 2)
        finish(*group, n % 2)

    def combine(r, carry):
        sl = pl.ds(pl.multiple_of(r * run_len, run_len), run_len)
        m0, m1, m2 = m_ref[0, sl, :], m_ref[1, sl, :], m_ref[2, sl, :]
        top = jnp.maximum(jnp.maximum(m0, m1), m2)
        w0, w1, w2 = jnp.exp(m0 - top), jnp.exp(m1 - top), jnp.exp(m2 - top)
        den = w0 * l_ref[0, sl, :] + w1 * l_ref[1, sl, :] + w2 * l_ref[2, sl, :]
        num = w0 * num_ref[0, sl, :] + w1 * num_ref[1, sl, :] + w2 * num_ref[2, sl, :]
        nat_ref[pl.ds(r, run_len, stride=MAX_DILATION), :] = num / den
        return carry
    lax.fori_loop(0, MAX_DILATION, combine, 0)

    rows = 256

    def emit(c, carry):
        sl = pl.ds(pl.multiple_of(c * rows, rows), rows)
        o_ref[sl, :] = nat_ref[sl, :].astype(BF16)
        return carry
    lax.fori_loop(0, (MAX_DILATION * run_len) // rows, emit, 0)


def _stage_scratch():
    out = []
    for branch in (0, 1):
        n_runs, width, back = BRANCHES[branch]
        shape = (2 * ATT_GROUP[branch], n_runs * width, n_runs * (width + back))
        out += [pltpu.VMEM(shape, F32), pltpu.VMEM(shape, BF16)]
    return out


def _attention(qkv, batch, seq):
    run_len = seq // MAX_DILATION
    qkv2 = qkv.reshape(batch, seq, 3 * D_ATT)
    biases = [jnp.asarray(_branch_bias(*b)) for b in BRANCHES]

    def head_block(offset):
        return pl.BlockSpec((None, seq, ATT_HEAD_DIM), lambda b, h: (b, 0, offset + h))

    def full(shape):
        return pl.BlockSpec(shape, lambda b, h: (0,) * len(shape))

    return pl.pallas_call(
        functools.partial(_attn_kernel, run_len=run_len),
        grid=(batch, ATT_HEADS),
        in_specs=[head_block(0), head_block(ATT_HEADS), head_block(2 * ATT_HEADS)]
        + [full(b.shape) for b in biases],
        out_specs=pl.BlockSpec((None, seq, ATT_HEAD_DIM), lambda b, h: (b, 0, h)),
        out_shape=jax.ShapeDtypeStruct((batch, seq, D_ATT), BF16),
        scratch_shapes=[
            pltpu.VMEM((len(BRANCHES), seq, ATT_HEAD_DIM), F32),
            pltpu.VMEM((len(BRANCHES), seq, ATT_HEAD_DIM), F32),
            pltpu.VMEM((len(BRANCHES), seq, ATT_HEAD_DIM), F32),
            pltpu.VMEM((seq, ATT_HEAD_DIM), F32),
        ] + _stage_scratch(),
        compiler_params=_params(("arbitrary", "arbitrary")),
        name="dilated_attn",
    )(qkv2, qkv2, qkv2, *biases)


OUT_TM = 512


def _outproj_kernel(ys_ref, ya_ref, w_ref, x_ref, gpost_ref, gpre_ref, h_ref, u_ref, mix_ref):
    mix_ref[...] = jnp.dot(ys_ref[...], w_ref[0:D_SSM, :], preferred_element_type=F32)
    mix_ref[...] += jnp.dot(ya_ref[...], w_ref[D_SSM:D_MIX, :], preferred_element_type=F32)
    chunk = 128

    def body(c, carry):
        sl = pl.ds(pl.multiple_of(c * chunk, chunk), chunk)
        h = x_ref[sl, :] + _rms_scale(mix_ref[sl, :], gpost_ref[...])
        h_ref[sl, :] = h
        u_ref[sl, :] = _rms_scale(h, gpre_ref[...]).astype(BF16)
        return carry
    lax.fori_loop(0, OUT_TM // chunk, body, 0)


def _out_projection(y_ssm, y_att, w_out, x2, g_post, g_pre):
    tokens = x2.shape[0]

    def rows(width):
        return pl.BlockSpec((OUT_TM, width), lambda i: (i, 0))

    def whole(shape):
        return pl.BlockSpec(shape, lambda i: (0, 0), pipeline_mode=pl.Buffered(1))

    return pl.pallas_call(
        _outproj_kernel,
        grid=(tokens // OUT_TM,),
        in_specs=[rows(D_SSM), rows(D_ATT), whole((D_MIX, D_MODEL)), rows(D_MODEL),
                  whole((1, D_MODEL)), whole((1, D_MODEL))],
        out_specs=[rows(D_MODEL), rows(D_MODEL)],
        out_shape=[
            jax.ShapeDtypeStruct((tokens, D_MODEL), F32),
            jax.ShapeDtypeStruct((tokens, D_MODEL), BF16),
        ],
        scratch_shapes=[pltpu.VMEM((OUT_TM, D_MODEL), F32)],
        compiler_params=_params(("arbitrary",)),
        name="outproj",
    )(y_ssm, y_att, w_out, x2, g_post, g_pre)


MLP_TM = 512
MLP_TF = 2048
MLP_PART = 512


def _mlp_kernel(u_ref, wup_ref, wdown_ref, h_ref, g_ref, o_ref):
    f = pl.program_id(1)

    @pl.when(f == 0)
    def _():
        o_ref[...] = jnp.zeros_like(o_ref)

    for part in range(MLP_TF // MLP_PART):
        cols = slice(part * MLP_PART, (part + 1) * MLP_PART)
        hid = jnp.maximum(jnp.dot(u_ref[...], wup_ref[:, cols], preferred_element_type=F32), 0.0)
        o_ref[...] += jnp.dot((hid * hid).astype(BF16), wdown_ref[cols, :], preferred_element_type=F32)

    @pl.when(f == pl.num_programs(1) - 1)
    def _():
        chunk = 128

        def body(c, carry):
            sl = pl.ds(pl.multiple_of(c * chunk, chunk), chunk)
            o_ref[sl, :] = h_ref[sl, :] + _rms_scale(o_ref[sl, :], g_ref[...])
            return carry
        lax.fori_loop(0, MLP_TM // chunk, body, 0)


def _mlp(u2, w_up, w_down, h1, g_post):
    tokens = u2.shape[0]
    return pl.pallas_call(
        _mlp_kernel,
        grid=(tokens // MLP_TM, D_FF // MLP_TF),
        in_specs=[
            pl.BlockSpec((MLP_TM, D_MODEL), lambda i, f: (i, 0)),
            pl.BlockSpec((D_MODEL, MLP_TF), lambda i, f: (0, f)),
            pl.BlockSpec((MLP_TF, D_MODEL), lambda i, f: (f, 0)),
            pl.BlockSpec((MLP_TM, D_MODEL), lambda i, f: (i, 0)),
            pl.BlockSpec((1, D_MODEL), lambda i, f: (0, 0)),
        ],
        out_specs=pl.BlockSpec((MLP_TM, D_MODEL), lambda i, f: (i, 0)),
        out_shape=jax.ShapeDtypeStruct((tokens, D_MODEL), F32),
        compiler_params=_params(("arbitrary", "arbitrary")),
        name="mlp",
    )(u2, w_up, w_down, h1, g_post)


def kernel(x, norm_mix_pre, w_in, conv_w, conv_b, dt_bias, a_log, d_skip, ssm_norm_w, w_out,
           norm_mix_post, norm_mlp_pre, w_up, w_down, norm_mlp_post):
    batch, seq, _ = x.shape
    depth = w_in.shape[0]
    h = x.reshape(batch * seq, D_MODEL)
    for layer in range(depth):
        w_main, w_dt = _prep_in_weights(w_in[layer])
        zxbc, dt_raw, qkv = _in_projection(
            h, norm_mix_pre[layer].reshape(1, D_MODEL), w_main, w_dt, batch, seq)
        y_ssm = _ssd(zxbc, dt_raw, conv_w[layer], conv_b[layer], dt_bias[layer], a_log[layer],
                     d_skip[layer], ssm_norm_w[layer], batch, seq)
        y_att = _attention(qkv, batch, seq).reshape(batch * seq, D_ATT)
        h1, u2 = _out_projection(
            y_ssm, y_att, w_out[layer].astype(BF16), h,
            norm_mix_post[layer].reshape(1, D_MODEL), norm_mlp_pre[layer].reshape(1, D_MODEL))
        h = _mlp(u2, w_up[layer].astype(BF16), w_down[layer].astype(BF16), h1,
                 norm_mlp_post[layer].reshape(1, D_MODEL))
    return h.reshape(batch, seq, D_MODEL)
```

```python
import functools

import numpy as np
import jax
import jax.numpy as jnp
from jax import lax
from jax.experimental import pallas as pl
from jax.experimental.pallas import tpu as pltpu

F32 = jnp.float32
BF16 = jnp.bfloat16

D_MODEL = 2048
SSM_HEAD_DIM = 64
SSM_HEADS = 32
SSM_GROUPS = 8
HEADS_PER_GROUP = 4
D_STATE = 128
D_SSM = SSM_HEADS * SSM_HEAD_DIM
D_BC = 2 * SSM_GROUPS * D_STATE
GROUP_WIDTH = HEADS_PER_GROUP * SSM_HEAD_DIM
CONV_WIDTH = 4
CHUNK = 128
ATT_HEADS = 16
ATT_HEAD_DIM = 128
D_ATT = ATT_HEADS * ATT_HEAD_DIM
D_MIX = D_SSM + D_ATT
D_FF = 4 * D_MODEL
WINDOW_REACH = 128
MAX_DILATION = 16
EPS = 1e-6
ATT_SCALE = ATT_HEAD_DIM ** -0.5
MASKED = -1e30
LOG2_E = 1.4426950408889634

LANES = 128
VMEM_LIMIT = 56 * 1024 * 1024


def _params(semantics):
    return pltpu.CompilerParams(dimension_semantics=semantics, vmem_limit_bytes=VMEM_LIMIT)


def _silu(v):
    half = 0.5 * v
    return half + half * jnp.tanh(half)


def _split3(v):
    hi = v.astype(BF16)
    rest = v - hi.astype(F32)
    mid = rest.astype(BF16)
    lo = (rest - mid.astype(F32)).astype(BF16)
    return hi, mid, lo


def _rms_scale(v, gain):
    ms = jnp.mean(v * v, axis=-1, keepdims=True)
    return (v * lax.rsqrt(ms + EPS)) * gain


PREP_COLS = 512


def _prep_kernel(wt_ref, wdt_ref, main_ref, dt_ref):
    main_ref[...] = wt_ref[...].T.astype(BF16)

    @pl.when(pl.program_id(0) == 0)
    def _():
        padded = jnp.concatenate([wdt_ref[...], jnp.zeros((LANES - SSM_HEADS, D_MODEL), F32)], axis=0)
        dt_ref[...] = padded.T.astype(BF16)


def _prep_in_weights(w):
    wt = w.T
    ssm_cols = 2 * D_SSM + D_BC
    att0 = ssm_cols + SSM_HEADS
    main_cols = ssm_cols + 3 * D_ATT
    n_ssm = ssm_cols // PREP_COLS

    def feature_row(j):
        return pl.multiple_of(jnp.where(j < n_ssm, j * PREP_COLS, att0 + (j - n_ssm) * PREP_COLS), 8)

    return pl.pallas_call(
        _prep_kernel,
        grid=(main_cols // PREP_COLS,),
        in_specs=[
            pl.BlockSpec((pl.Element(PREP_COLS), pl.Element(D_MODEL)), lambda j: (feature_row(j), 0)),
            pl.BlockSpec((pl.Element(SSM_HEADS), pl.Element(D_MODEL)), lambda j: (ssm_cols, 0)),
        ],
        out_specs=[pl.BlockSpec((D_MODEL, PREP_COLS), lambda j: (0, j)),
                   pl.BlockSpec((D_MODEL, LANES), lambda j: (0, 0))],
        out_shape=[jax.ShapeDtypeStruct((D_MODEL, main_cols), BF16),
                   jax.ShapeDtypeStruct((D_MODEL, LANES), BF16)],
        compiler_params=_params(("arbitrary",)),
        name="inproj_weights",
    )(wt, wt)


IN_TM = 1024
IN_TN = 1024
IN_STAGE_SLABS = 4
IN_STAGE_PITCH = 24


def _inproj_kernel(x_ref, g_ref, w_ref, wdt_ref, zxbc_ref, dt_ref, qkv_ref, u_ref, uperm_ref, uf_ref,
                   *, n_ssm_tiles):
    j = pl.program_id(1)
    chunk = 128
    n_slabs = D_MODEL // LANES
    rows = IN_TM // MAX_DILATION

    @pl.when(j == 0)
    def _():
        def body(c, carry):
            sl = pl.ds(pl.multiple_of(c * chunk, chunk), chunk)
            u_ref[sl, :] = _rms_scale(x_ref[sl, :], g_ref[...]).astype(BF16)
            return carry
        lax.fori_loop(0, IN_TM // chunk, body, 0)
        dt_ref[...] = jnp.dot(u_ref[...], wdt_ref[...], preferred_element_type=F32)
        for s0 in range(0, n_slabs, IN_STAGE_SLABS):
            for s in range(IN_STAGE_SLABS):
                cols = slice((s0 + s) * LANES, (s0 + s + 1) * LANES)
                for grp in range(rows):
                    uf_ref[s, grp * IN_STAGE_PITCH:grp * IN_STAGE_PITCH + MAX_DILATION, :] = (
                        u_ref[grp * MAX_DILATION:(grp + 1) * MAX_DILATION, cols].astype(F32))
            for r in range(MAX_DILATION):
                for s in range(IN_STAGE_SLABS):
                    uperm_ref[r * rows:(r + 1) * rows, (s0 + s) * LANES:(s0 + s + 1) * LANES] = (
                        uf_ref[s, pl.ds(r, rows, stride=IN_STAGE_PITCH), :].astype(BF16))

    @pl.when(j < n_ssm_tiles)
    def _():
        zxbc_ref[...] = jnp.dot(u_ref[...], w_ref[...], preferred_element_type=F32)

    @pl.when(j >= n_ssm_tiles)
    def _():
        scale = jnp.where(j < n_ssm_tiles + D_ATT // IN_TN, ATT_SCALE, 1.0).astype(F32)
        res = jnp.dot(uperm_ref[...], w_ref[...], preferred_element_type=F32) * scale
        for r in range(MAX_DILATION):
            qkv_ref[r] = res[r * rows:(r + 1) * rows, :].astype(BF16)


def _in_projection(x2, gain, w_main, w_dt, batch, seq):
    tokens = x2.shape[0]
    n_m = tokens // IN_TM
    m_per_seq = seq // IN_TM
    ssm_cols = 2 * D_SSM + D_BC
    att_cols = 3 * D_ATT
    n_ssm_tiles = ssm_cols // IN_TN
    n_att_tiles = att_cols // IN_TN
    run_len = seq // MAX_DILATION
    return pl.pallas_call(
        functools.partial(_inproj_kernel, n_ssm_tiles=n_ssm_tiles),
        grid=(n_m, n_ssm_tiles + n_att_tiles),
        in_specs=[
            pl.BlockSpec((IN_TM, D_MODEL), lambda i, j: (i, 0)),
            pl.BlockSpec((1, D_MODEL), lambda i, j: (0, 0)),
            pl.BlockSpec((D_MODEL, IN_TN), lambda i, j: (0, j)),
            pl.BlockSpec((D_MODEL, LANES), lambda i, j: (0, 0)),
        ],
        out_specs=[
            pl.BlockSpec((IN_TM, IN_TN), lambda i, j: (i, jnp.minimum(j, n_ssm_tiles - 1))),
            pl.BlockSpec((IN_TM, LANES), lambda i, j: (i, 0)),
            pl.BlockSpec((None, MAX_DILATION, IN_TM // MAX_DILATION, IN_TN),
                         lambda i, j: (i // m_per_seq, 0, i % m_per_seq, jnp.maximum(j - n_ssm_tiles, 0))),
        ],
        out_shape=[
            jax.ShapeDtypeStruct((tokens, ssm_cols), F32),
            jax.ShapeDtypeStruct((tokens, LANES), F32),
            jax.ShapeDtypeStruct((batch, MAX_DILATION, run_len, att_cols), BF16),
        ],
        scratch_shapes=[
            pltpu.VMEM((IN_TM, D_MODEL), BF16),
            pltpu.VMEM((IN_TM, D_MODEL), BF16),
            pltpu.VMEM((IN_STAGE_SLABS, IN_TM // MAX_DILATION * IN_STAGE_PITCH, LANES), F32),
        ],
        compiler_params=_params(("arbitrary", "arbitrary")),
        name="inproj",
    )(x2, gain, w_main, w_dt)


CONV_PAD = 8
SSD_ROWS = 2 * CHUNK


def _ssd_kernel(z_ref, x_ref, bc_ref, dt_ref, cwx_ref, cbx_ref, cwbc_ref, cbbc_ref,
                dtb_ref, alog_ref, dsk_ref, nw_ref, tri_ref, expand_ref,
                y_ref, state_ref, xpad_ref, bcpad_ref):
    @pl.when(pl.program_id(1) == 0)
    def _():
        state_ref[...] = jnp.zeros_like(state_ref)
        xpad_ref[0:CONV_PAD, :] = jnp.zeros((CONV_PAD, D_SSM), F32)
        bcpad_ref[0:CONV_PAD, :] = jnp.zeros((CONV_PAD, D_BC), F32)

    xpad_ref[CONV_PAD:CONV_PAD + SSD_ROWS, :] = x_ref[...]
    bcpad_ref[CONV_PAD:CONV_PAD + SSD_ROWS, :] = bc_ref[...]

    def conv_silu(pad_ref, w_ref, b_ref, r0, c0, width):
        cols = slice(c0, c0 + width)
        acc = b_ref[:, cols]
        for k in range(CONV_WIDTH):
            start = r0 + CONV_PAD - (CONV_WIDTH - 1) + k
            acc = acc + w_ref[k:k + 1, cols] * pad_ref[start:start + CHUNK, cols]
        return _silu(acc)

    tri = tri_ref[...]
    expand = expand_ref[...]
    row_i = lax.broadcasted_iota(jnp.int32, (CHUNK, CHUNK), 0)
    col_j = lax.broadcasted_iota(jnp.int32, (CHUNK, CHUNK), 1)
    causal = row_i >= col_j
    lane_head = lax.broadcasted_iota(jnp.int32, (CHUNK, GROUP_WIDTH), 1) // SSM_HEAD_DIM
    own_lanes = [jnp.where(lane_head == r, 1.0, 0.0).astype(BF16) for r in range(HEADS_PER_GROUP)]

    for r0 in range(0, SSD_ROWS, CHUNK):
        rows = slice(r0, r0 + CHUNK)
        dt_in = dt_ref[rows, :] + dtb_ref[...]
        dtv = jnp.maximum(dt_in, 0.0) + jnp.log1p(jnp.exp(-jnp.abs(dt_in)))
        da = dtv * (-jnp.exp(alog_ref[...]))
        a_cs = sum(jnp.dot(tri, part, preferred_element_type=F32)
                   for part in _split3(da))
        a_last = a_cs[CHUNK - 1:CHUNK, :]
        w_end = dtv * jnp.exp(a_last - a_cs)
        a_cs2 = a_cs * LOG2_E
        a_cs_t = a_cs2.T
        dt_t = dtv.T
        w_end_t = w_end.T
        chunk_decay = sum(jnp.dot(part, expand, preferred_element_type=F32)
                          for part in _split3(jnp.broadcast_to(jnp.exp(a_last), (8, LANES)))
                          )[0:1, :]

        for g in range(SSM_GROUPS):
            gcols = slice(g * GROUP_WIDTH, (g + 1) * GROUP_WIDTH)
            xs = conv_silu(xpad_ref, cwx_ref, cbx_ref, r0, g * GROUP_WIDTH, GROUP_WIDTH)
            bm = conv_silu(bcpad_ref, cwbc_ref, cbbc_ref, r0, g * D_STATE, D_STATE)
            cm = conv_silu(bcpad_ref, cwbc_ref, cbbc_ref, r0, D_BC // 2 + g * D_STATE, D_STATE)
            cb = lax.dot_general(cm.astype(BF16), bm.astype(BF16), (((1,), (1,)), ((), ())),
                                 preferred_element_type=F32)
            bm_t = bm.T
            prev = state_ref[g]
            xs16 = xs.astype(BF16)
            prev16 = prev.astype(BF16)

            lhs_y, rhs_y, lhs_s, rhs_s = [], [], [], []
            for r in range(HEADS_PER_GROUP):
                h = g * HEADS_PER_GROUP + r
                col_a = jnp.broadcast_to(a_cs2[:, h:h + 1], (CHUNK, CHUNK))
                row_a = a_cs_t[h:h + 1, :]
                decay = jnp.exp2(jnp.where(causal, col_a - row_a, -jnp.inf))
                lhs_y.append((cb * decay * dt_t[h:h + 1, :]).astype(BF16))
                lhs_y.append((cm * jnp.exp2(col_a)).astype(BF16))
                x_own = xs16 * own_lanes[r]
                rhs_y.append(x_own)
                rhs_y.append(prev16 * own_lanes[r])
                lhs_s.append((bm_t * w_end_t[h:h + 1, :]).astype(BF16))
                rhs_s.append(x_own)

            y = jnp.dot(jnp.concatenate(lhs_y, axis=1), jnp.concatenate(rhs_y, axis=0),
                        preferred_element_type=F32)
            s_new = jnp.dot(jnp.concatenate(lhs_s, axis=1), jnp.concatenate(rhs_s, axis=0),
                            preferred_element_type=F32)
            state_ref[g] = prev * chunk_decay[:, gcols] + s_new

            y = y + dsk_ref[:, gcols] * xs
            gated = y * _silu(z_ref[rows, gcols])
            y_ref[rows, gcols] = _rms_scale(gated, nw_ref[:, gcols]).astype(BF16)

    xpad_ref[0:CONV_PAD, :] = x_ref[SSD_ROWS - CONV_PAD:SSD_ROWS, :]
    bcpad_ref[0:CONV_PAD, :] = bc_ref[SSD_ROWS - CONV_PAD:SSD_ROWS, :]


def _ssd(zxbc, dt_raw, conv_w, conv_b, dt_bias, a_log, d_skip, ssm_norm_w, batch, seq):
    tokens = zxbc.shape[0]
    n_steps = seq // SSD_ROWS
    pad_heads = LANES - SSM_HEADS
    dtb = jnp.pad(dt_bias.astype(F32), (0, pad_heads)).reshape(1, LANES)
    alog = jnp.pad(a_log.astype(F32), (0, pad_heads)).reshape(1, LANES)
    dsk = jnp.repeat(d_skip.astype(F32), SSM_HEAD_DIM).reshape(1, D_SSM)
    tri = jnp.asarray(np.tril(np.ones((CHUNK, CHUNK), np.float32)), dtype=BF16)
    expand = np.zeros((LANES, D_SSM), np.float32)
    for h in range(SSM_HEADS):
        expand[h, h * SSM_HEAD_DIM:(h + 1) * SSM_HEAD_DIM] = 1.0
    expand = jnp.asarray(expand, dtype=BF16)

    def rows(b, c):
        return b * n_steps + c

    def full(shape):
        return pl.BlockSpec(shape, lambda b, c: (0,) * len(shape))

    return pl.pallas_call(
        _ssd_kernel,
        grid=(batch, n_steps),
        in_specs=[
            pl.BlockSpec((SSD_ROWS, D_SSM), lambda b, c: (rows(b, c), 0)),
            pl.BlockSpec((SSD_ROWS, D_SSM), lambda b, c: (rows(b, c), 1)),
            pl.BlockSpec((SSD_ROWS, D_BC), lambda b, c: (rows(b, c), 2)),
            pl.BlockSpec((SSD_ROWS, LANES), lambda b, c: (rows(b, c), 0)),
            full((CONV_WIDTH, D_SSM)), full((1, D_SSM)),
            full((CONV_WIDTH, D_BC)), full((1, D_BC)),
            full((1, LANES)), full((1, LANES)), full((1, D_SSM)), full((1, D_SSM)),
            full((CHUNK, CHUNK)), full((LANES, D_SSM)),
        ],
        out_specs=pl.BlockSpec((SSD_ROWS, D_SSM), lambda b, c: (rows(b, c), 0)),
        out_shape=jax.ShapeDtypeStruct((tokens, D_SSM), BF16),
        scratch_shapes=[
            pltpu.VMEM((SSM_GROUPS, D_STATE, GROUP_WIDTH), F32),
            pltpu.VMEM((CONV_PAD + SSD_ROWS, D_SSM), F32),
            pltpu.VMEM((CONV_PAD + SSD_ROWS, D_BC), F32),
        ],
        compiler_params=_params(("arbitrary", "arbitrary")),
        name="ssd",
    )(zxbc, zxbc, zxbc, dt_raw,
      conv_w[:, :D_SSM], conv_b[:D_SSM].reshape(1, D_SSM),
      conv_w[:, D_SSM:], conv_b[D_SSM:].reshape(1, D_BC),
      dtb, alog, dsk, ssm_norm_w.reshape(1, D_SSM), tri, expand)


BRANCHES = ((16, 16, 16), (4, 32, 32), (1, 128, 128))
ATT_GROUP = (3, 6, 6)
assert ATT_GROUP[1] == ATT_GROUP[2]


def _branch_bias(n_runs, width, back):
    s_q = np.repeat(np.arange(n_runs), width)
    i_q = np.tile(np.arange(width), n_runs)
    j_q = n_runs * i_q + s_q
    s_p = np.repeat(np.arange(n_runs), back)
    i_p = np.tile(np.arange(back), n_runs) - back
    j_k = np.concatenate([n_runs * i_p + s_p, j_q])
    dist = j_q[:, None] - j_k[None, :]
    return np.where((dist >= 0) & (dist <= WINDOW_REACH), 0.0, MASKED).astype(np.float32)


def _attn_kernel(q_ref, k_ref, v_ref, b0_ref, b1_ref, b2_ref, o_ref,
                 num_ref, m_ref, l_ref, nat_ref, s0_ref, p0_ref, s1_ref, p1_ref, *, run_len):
    bias_refs = (b0_ref, b1_ref, b2_ref)
    stage_refs = ((s0_ref, p0_ref), (s1_ref, p1_ref), (s1_ref, p1_ref))

    def rows_of(ref, starts, size):
        parts = [ref[st:st + size, :] for st in starts]
        return parts[0] if len(parts) == 1 else jnp.concatenate(parts, axis=0)

    def keys_of(ref, branch, starts, with_prev):
        n_runs, width, back = BRANCHES[branch]
        cur = rows_of(ref, starts, width)
        if not with_prev:
            return cur
        return jnp.concatenate([rows_of(ref, [st - back for st in starts], back), cur], axis=0)

    def layout(branch, blocks, slot):
        n_runs, width, back = BRANCHES[branch]
        n_cls = MAX_DILATION // n_runs
        out = []
        for u, (cls, i0, with_prev) in enumerate(blocks):
            starts = [(cls + n_cls * s) * run_len + i0 for s in range(n_runs)]
            out.append((slot * ATT_GROUP[branch] + u, starts, with_prev,
                        n_runs * (width + (back if with_prev else 0))))
        return out

    def issue(branch, blocks, slot):
        n_runs, width, back = BRANCHES[branch]
        s_ref, _ = stage_refs[branch]
        for idx, starts, with_prev, n_keys in layout(branch, blocks, slot):
            q = rows_of(q_ref, starts, width)
            k = keys_of(k_ref, branch, starts, with_prev)
            bias = bias_refs[branch][...] if with_prev else bias_refs[branch][:, n_runs * back:]
            s_ref[idx, :, 0:n_keys] = lax.dot_general(
                q, k, (((1,), (1,)), ((), ())), preferred_element_type=F32) + bias

    def finish(branch, blocks, slot):
        n_runs, width, back = BRANCHES[branch]
        s_ref, p_ref = stage_refs[branch]
        plan = layout(branch, blocks, slot)
        for idx, starts, with_prev, n_keys in plan:
            s = s_ref[idx, :, 0:n_keys]
            m = jnp.max(s, axis=-1, keepdims=True)
            p_ref[idx, :, 0:n_keys] = jnp.exp(s - m).astype(BF16)
            for part, st in enumerate(starts):
                m_ref[branch, st:st + width, :] = jnp.broadcast_to(
                    m[part * width:(part + 1) * width], (width, ATT_HEAD_DIM))
        for idx, starts, with_prev, n_keys in plan:
            v = keys_of(v_ref, branch, starts, with_prev)
            v1 = jnp.concatenate([v, jnp.ones((n_keys, ATT_HEAD_DIM), BF16)], axis=1)
            pv = jnp.dot(p_ref[idx, :, 0:n_keys], v1, preferred_element_type=F32)
            for part, st in enumerate(starts):
                rows = slice(part * width, (part + 1) * width)
                num_ref[branch, st:st + width, :] = pv[rows, 0:ATT_HEAD_DIM]
                l_ref[branch, st:st + width, :] = pv[rows, ATT_HEAD_DIM:]

    groups = []
    for branch, (n_runs, width, back) in enumerate(BRANCHES):
        blocks = [(cls, i0, i0 > 0) for cls in range(MAX_DILATION // n_runs) for i0 in range(0, run_len, width)]
        size = ATT_GROUP[branch]
        groups += [(branch, blocks[g:g + size]) for g in range(0, len(blocks), size)]
    issue(*groups[0], 0)
    for n, group in enumerate(groups):
        if n + 1 < len(groups):
            issue(*groups[n + 1], (n + 1) % 2)
        finish(*group, n % 2)

    def combine(r, carry):
        sl = pl.ds(pl.multiple_of(r * run_len, run_len), run_len)
        m0, m1, m2 = m_ref[0, sl, :], m_ref[1, sl, :], m_ref[2, sl, :]
        top = jnp.maximum(jnp.maximum(m0, m1), m2)
        w0, w1, w2 = jnp.exp(m0 - top), jnp.exp(m1 - top), jnp.exp(m2 - top)
        den = w0 * l_ref[0, sl, :] + w1 * l_ref[1, sl, :] + w2 * l_ref[2, sl, :]
        num = w0 * num_ref[0, sl, :] + w1 * num_ref[1, sl, :] + w2 * num_ref[2, sl, :]
        nat_ref[pl.ds(r, run_len, stride=MAX_DILATION), :] = num / den
        return carry
    lax.fori_loop(0, MAX_DILATION, combine, 0)

    rows = 256

    def emit(c, carry):
        sl = pl.ds(pl.multiple_of(c * rows, rows), rows)
        o_ref[sl, :] = nat_ref[sl, :].astype(BF16)
        return carry
    lax.fori_loop(0, (MAX_DILATION * run_len) // rows, emit, 0)


def _stage_scratch():
    out = []
    for branch in (0, 1):
        n_runs, width, back = BRANCHES[branch]
        shape = (2 * ATT_GROUP[branch], n_runs * width, n_runs * (width + back))
        out += [pltpu.VMEM(shape, F32), pltpu.VMEM(shape, BF16)]
    return out


def _attention(qkv, batch, seq):
    run_len = seq // MAX_DILATION
    qkv2 = qkv.reshape(batch, seq, 3 * D_ATT)
    biases = [jnp.asarray(_branch_bias(*b)) for b in BRANCHES]

    def head_block(offset):
        return pl.BlockSpec((None, seq, ATT_HEAD_DIM), lambda b, h: (b, 0, offset + h))

    def full(shape):
        return pl.BlockSpec(shape, lambda b, h: (0,) * len(shape))

    return pl.pallas_call(
        functools.partial(_attn_kernel, run_len=run_len),
        grid=(batch, ATT_HEADS),
        in_specs=[head_block(0), head_block(ATT_HEADS), head_block(2 * ATT_HEADS)]
        + [full(b.shape) for b in biases],
        out_specs=pl.BlockSpec((None, seq, ATT_HEAD_DIM), lambda b, h: (b, 0, h)),
        out_shape=jax.ShapeDtypeStruct((batch, seq, D_ATT), BF16),
        scratch_shapes=[
            pltpu.VMEM((len(BRANCHES), seq, ATT_HEAD_DIM), F32),
            pltpu.VMEM((len(BRANCHES), seq, ATT_HEAD_DIM), F32),
            pltpu.VMEM((len(BRANCHES), seq, ATT_HEAD_DIM), F32),
            pltpu.VMEM((seq, ATT_HEAD_DIM), F32),
        ] + _stage_scratch(),
        compiler_params=_params(("arbitrary", "arbitrary")),
        name="dilated_attn",
    )(qkv2, qkv2, qkv2, *biases)


OUT_TM = 512


def _outproj_kernel(ys_ref, ya_ref, w_ref, x_ref, gpost_ref, gpre_ref, h_ref, u_ref, mix_ref):
    mix_ref[...] = jnp.dot(ys_ref[...], w_ref[0:D_SSM, :], preferred_element_type=F32)
    mix_ref[...] += jnp.dot(ya_ref[...], w_ref[D_SSM:D_MIX, :], preferred_element_type=F32)
    chunk = 128

    def body(c, carry):
        sl = pl.ds(pl.multiple_of(c * chunk, chunk), chunk)
        h = x_ref[sl, :] + _rms_scale(mix_ref[sl, :], gpost_ref[...])
        h_ref[sl, :] = h
        u_ref[sl, :] = _rms_scale(h, gpre_ref[...]).astype(BF16)
        return carry
    lax.fori_loop(0, OUT_TM // chunk, body, 0)


def _out_projection(y_ssm, y_att, w_out, x2, g_post, g_pre):
    tokens = x2.shape[0]

    def rows(width):
        return pl.BlockSpec((OUT_TM, width), lambda i: (i, 0))

    def whole(shape):
        return pl.BlockSpec(shape, lambda i: (0, 0), pipeline_mode=pl.Buffered(1))

    return pl.pallas_call(
        _outproj_kernel,
        grid=(tokens // OUT_TM,),
        in_specs=[rows(D_SSM), rows(D_ATT), whole((D_MIX, D_MODEL)), rows(D_MODEL),
                  whole((1, D_MODEL)), whole((1, D_MODEL))],
        out_specs=[rows(D_MODEL), rows(D_MODEL)],
        out_shape=[
            jax.ShapeDtypeStruct((tokens, D_MODEL), F32),
            jax.ShapeDtypeStruct((tokens, D_MODEL), BF16),
        ],
        scratch_shapes=[pltpu.VMEM((OUT_TM, D_MODEL), F32)],
        compiler_params=_params(("arbitrary",)),
        name="outproj",
    )(y_ssm, y_att, w_out, x2, g_post, g_pre)


MLP_TM = 512
MLP_TF = 2048
MLP_PART = 512


def _mlp_kernel(u_ref, wup_ref, wdown_ref, h_ref, g_ref, o_ref):
    f = pl.program_id(1)

    @pl.when(f == 0)
    def _():
        o_ref[...] = jnp.zeros_like(o_ref)

    for part in range(MLP_TF // MLP_PART):
        cols = slice(part * MLP_PART, (part + 1) * MLP_PART)
        hid = jnp.maximum(jnp.dot(u_ref[...], wup_ref[:, cols], preferred_element_type=F32), 0.0)
        o_ref[...] += jnp.dot((hid * hid).astype(BF16), wdown_ref[cols, :], preferred_element_type=F32)

    @pl.when(f == pl.num_programs(1) - 1)
    def _():
        chunk = 128

        def body(c, carry):
            sl = pl.ds(pl.multiple_of(c * chunk, chunk), chunk)
            o_ref[sl, :] = h_ref[sl, :] + _rms_scale(o_ref[sl, :], g_ref[...])
            return carry
        lax.fori_loop(0, MLP_TM // chunk, body, 0)


def _mlp(u2, w_up, w_down, h1, g_post):
    tokens = u2.shape[0]
    return pl.pallas_call(
        _mlp_kernel,
        grid=(tokens // MLP_TM, D_FF // MLP_TF),
        in_specs=[
            pl.BlockSpec((MLP_TM, D_MODEL), lambda i, f: (i, 0)),
            pl.BlockSpec((D_MODEL, MLP_TF), lambda i, f: (0, f)),
            pl.BlockSpec((MLP_TF, D_MODEL), lambda i, f: (f, 0)),
            pl.BlockSpec((MLP_TM, D_MODEL), lambda i, f: (i, 0)),
            pl.BlockSpec((1, D_MODEL), lambda i, f: (0, 0)),
        ],
        out_specs=pl.BlockSpec((MLP_TM, D_MODEL), lambda i, f: (i, 0)),
        out_shape=jax.ShapeDtypeStruct((tokens, D_MODEL), F32),
        compiler_params=_params(("arbitrary", "arbitrary")),
        name="mlp",
    )(u2, w_up, w_down, h1, g_post)


def kernel(x, norm_mix_pre, w_in, conv_w, conv_b, dt_bias, a_log, d_skip, ssm_norm_w, w_out,
           norm_mix_post, norm_mlp_pre, w_up, w_down, norm_mlp_post):
    batch, seq, _ = x.shape
    depth = w_in.shape[0]
    h = x.reshape(batch * seq, D_MODEL)
    for layer in range(depth):
        w_main, w_dt = _prep_in_weights(w_in[layer])
        zxbc, dt_raw, qkv = _in_projection(
            h, norm_mix_pre[layer].reshape(1, D_MODEL), w_main, w_dt, batch, seq)
        y_ssm = _ssd(zxbc, dt_raw, conv_w[layer], conv_b[layer], dt_bias[layer], a_log[layer],
                     d_skip[layer], ssm_norm_w[layer], batch, seq)
        y_att = _attention(qkv, batch, seq).reshape(batch * seq, D_ATT)
        h1, u2 = _out_projection(
            y_ssm, y_att, w_out[layer].astype(BF16), h,
            norm_mix_post[layer].reshape(1, D_MODEL), norm_mlp_pre[layer].reshape(1, D_MODEL))
        h = _mlp(u2, w_up[layer].astype(BF16), w_down[layer].astype(BF16), h1,
                 norm_mlp_post[layer].reshape(1, D_MODEL))
    return h.reshape(batch, seq, D_MODEL)
```
